```python
import math
import jax, jax.numpy as jnp
from jax import lax
import numpy as np

D_MODEL = 1024
BATCH = 16
SEQ = 2048
DEPTH = 2

ATT_HEADS = 8
ATT_HEAD_DIM = 64
ATT_WIDTH = ATT_HEADS * ATT_HEAD_DIM
Q_BLOCK = 128
SSM_GROUP = 16
SSM_WIDTH = D_MODEL // 2
SSM_GROUPS = SSM_WIDTH // SSM_GROUP
SSM_STATE = 64
DT_MIN = 1e-3
DT_MAX = 1e-1
D_FF = 256 * ((8 * D_MODEL // 3 + 255) // 256)
N_EXPERTS = 8
TOP_K = 2
N_DENSE = (DEPTH + 1) // 2
N_MOE = DEPTH // 2
N_IN = 3 * ATT_WIDTH + ATT_HEADS + SSM_WIDTH + 2 * D_MODEL
EPS = 1e-6

kernel_name = 'fox_s5_gated_hybrid_moe_block'


def _rmsnorm(x, g):
    xf = x.astype(jnp.float32)
    y = xf * lax.rsqrt(jnp.mean(xf * xf, axis=-1, keepdims=True) + EPS)
    return (y * g.astype(jnp.float32)).astype(x.dtype)


def _forgetting_attention(q, k, v, log_f):
    bsz, seq, heads, dh = q.shape
    nblk = seq // Q_BLOCK
    cum = jnp.cumsum(log_f, axis=1).transpose(0, 2, 1)
    kh = k.transpose(0, 2, 1, 3)
    vh = v.transpose(0, 2, 1, 3)
    q_blocks = q.transpose(0, 2, 1, 3).reshape(bsz, heads, nblk, Q_BLOCK, dh).transpose(2, 0, 1, 3, 4)
    c_blocks = cum.reshape(bsz, heads, nblk, Q_BLOCK).transpose(2, 0, 1, 3)
    starts = jnp.arange(nblk, dtype=jnp.int32) * Q_BLOCK
    key_pos = jnp.arange(seq, dtype=jnp.int32)
    scale = dh ** -0.5

    def one_block(args):
        qb, cb, s0 = args
        logits = jnp.einsum('bhqd,bhkd->bhqk', qb, kh, preferred_element_type=jnp.float32) * scale
        logits = logits + cb[..., None] - cum[:, :, None, :]
        q_pos = s0 + jnp.arange(Q_BLOCK, dtype=jnp.int32)
        causal = q_pos[:, None] >= key_pos[None, :]
        logits = jnp.where(causal, logits, -jnp.inf)
        p = jax.nn.softmax(logits, axis=-1)
        return jnp.einsum('bhqk,bhkd->bhqd', p.astype(vh.dtype), vh)

    out = lax.map(one_block, (q_blocks, c_blocks, starts))
    return out.transpose(1, 0, 3, 2, 4).reshape(bsz, seq, heads * dh)


def _s5(u, lam_re, lam_im, log_dt, b_re, b_im, c_re, c_im, d_skip, w_glu, b_glu):
    bsz, seq, _ = u.shape
    f32 = jnp.float32
    uf = u.astype(f32).reshape(bsz, seq, SSM_GROUPS, SSM_GROUP)
    lr, li = lam_re.astype(f32), lam_im.astype(f32)
    dt = jnp.exp(log_dt.astype(f32))[:, None]
    mag = jnp.exp(lr * dt)
    ang = li * dt
    a_re, a_im = mag * jnp.cos(ang), mag * jnp.sin(ang)
    n_re, n_im = a_re - 1.0, a_im
    den = lr * lr + li * li
    k_re = (n_re * lr + n_im * li) / den
    k_im = (n_im * lr - n_re * li) / den
    br, bi = b_re.astype(f32), b_im.astype(f32)
    bb_re = k_re[..., None] * br - k_im[..., None] * bi
    bb_im = k_re[..., None] * bi + k_im[..., None] * br
    bu_re = jnp.einsum('blgh,gph->blgp', uf, bb_re)
    bu_im = jnp.einsum('blgh,gph->blgp', uf, bb_im)
    a_re_t = jnp.broadcast_to(a_re, (1, seq, SSM_GROUPS, SSM_STATE))
    a_im_t = jnp.broadcast_to(a_im, (1, seq, SSM_GROUPS, SSM_STATE))

    def combine(e1, e2):
        a1r, a1i, b1r, b1i = e1
        a2r, a2i, b2r, b2i = e2
        return (a2r * a1r - a2i * a1i,
                a2r * a1i + a2i * a1r,
                a2r * b1r - a2i * b1i + b2r,
                a2r * b1i + a2i * b1r + b2i)

    _, _, s_re, s_im = lax.associative_scan(combine, (a_re_t, a_im_t, bu_re, bu_im), axis=1)
    y = (jnp.einsum('blgp,ghp->blgh', s_re, c_re.astype(f32))
         - jnp.einsum('blgp,ghp->blgh', s_im, c_im.astype(f32))
         + d_skip.astype(f32) * uf)
    y = jax.nn.gelu(y.reshape(bsz, seq, SSM_WIDTH))
    y = y * jax.nn.sigmoid(y @ w_glu.astype(f32) + b_glu.astype(f32))
    return y.astype(u.dtype)


def _swiglu(t, w1, w3, w2):
    return (jax.nn.silu(t @ w1) * (t @ w3)) @ w2


def _moe(h, w_router, b_router, w1, w3, w2):
    bsz, seq, d = h.shape
    t = h.reshape(-1, d)
    logits = (t @ w_router + b_router).astype(jnp.float32)
    top_val, top_idx = lax.top_k(logits, TOP_K)
    top_w = jax.nn.softmax(top_val, axis=-1)
    gates = jnp.sum(jax.nn.one_hot(top_idx, N_EXPERTS, dtype=jnp.float32) * top_w[..., None], axis=1)
    out = jnp.zeros_like(t)
    for e in range(N_EXPERTS):
        out = out + gates[:, e:e + 1].astype(t.dtype) * _swiglu(t, w1[e], w3[e], w2[e])
    return out.reshape(bsz, seq, d)


def setup_inputs(seed: int = 0) -> dict:
    key = jax.random.key(seed)
    ks = jax.random.split(key, 32)
    f32 = jnp.float32
    D = D_MODEL
    G, P, Hc = SSM_GROUPS, SSM_STATE, SSM_GROUP

    def nrm(k, shape, scale):
        return jax.random.normal(k, shape, f32) * scale

    n_idx = jnp.arange(P, dtype=f32)
    return {
        'x': nrm(ks[0], (BATCH, SEQ, D), 1.0),
        'c': nrm(ks[1], (BATCH, D), 1.0),
        'w_ada': nrm(ks[2], (DEPTH, D, 6 * D), D ** -0.5),
        'b_ada': nrm(ks[3], (DEPTH, 6 * D), 0.02),
        'g_mix': 1.0 + nrm(ks[4], (DEPTH, D), 0.05),
        'w_in': nrm(ks[5], (DEPTH, D, N_IN), D ** -0.5),
        'b_forget': 2.0 + nrm(ks[6], (DEPTH, ATT_HEADS), 0.5),
        'b_gate': nrm(ks[7], (DEPTH, 2 * D), 0.02),
        'lam_re': -0.5 + nrm(ks[8], (DEPTH, G, P), 0.01),
        'lam_im': math.pi * n_idx + nrm(ks[9], (DEPTH, G, P), 0.01),
        'log_dt': jax.random.uniform(ks[10], (DEPTH, G), f32, math.log(DT_MIN), math.log(DT_MAX)),
        'b_re': nrm(ks[11], (DEPTH, G, P, Hc), (2 * Hc) ** -0.5),
        'b_im': nrm(ks[12], (DEPTH, G, P, Hc), (2 * Hc) ** -0.5),
        'c_re': nrm(ks[13], (DEPTH, G, Hc, P), P ** -0.5),
        'c_im': nrm(ks[14], (DEPTH, G, Hc, P), P ** -0.5),
        'd_skip': nrm(ks[15], (DEPTH, G, Hc), 1.0),
        'w_glu': nrm(ks[16], (DEPTH, SSM_WIDTH, SSM_WIDTH), SSM_WIDTH ** -0.5),
        'b_glu': nrm(ks[17], (DEPTH, SSM_WIDTH), 0.02),
        'w_proj_att': nrm(ks[18], (DEPTH, ATT_WIDTH, D), ATT_WIDTH ** -0.5),
        'w_proj_ssm': nrm(ks[19], (DEPTH, SSM_WIDTH, D), SSM_WIDTH ** -0.5),
        'w_out': nrm(ks[20], (DEPTH, D, D), D ** -0.5),
        'g_ffn': 1.0 + nrm(ks[21], (DEPTH, D), 0.05),
        'w1_dense': nrm(ks[22], (N_DENSE, D, D_FF), D ** -0.5),
        'w3_dense': nrm(ks[23], (N_DENSE, D, D_FF), D ** -0.5),
        'w2_dense': nrm(ks[24], (N_DENSE, D_FF, D), D_FF ** -0.5),
        'w_router': nrm(ks[25], (N_MOE, D, N_EXPERTS), D ** -0.5),
        'b_router': nrm(ks[26], (N_MOE, N_EXPERTS), 0.01),
        'w1_moe': nrm(ks[27], (N_MOE, N_EXPERTS, D, D_FF), D ** -0.5),
        'w3_moe': nrm(ks[28], (N_MOE, N_EXPERTS, D, D_FF), D ** -0.5),
        'w2_moe': nrm(ks[29], (N_MOE, N_EXPERTS, D_FF, D), D_FF ** -0.5),
        'g_final': 1.0 + nrm(ks[30], (D,), 0.05),
    }


def reference(x, c, w_ada, b_ada, g_mix, w_in, b_forget, b_gate, lam_re, lam_im, log_dt,
              b_re, b_im, c_re, c_im, d_skip, w_glu, b_glu, w_proj_att, w_proj_ssm, w_out,
              g_ffn, w1_dense, w3_dense, w2_dense, w_router, b_router, w1_moe, w3_moe, w2_moe,
              g_final):
    bsz, seq, d = x.shape
    cond = jax.nn.silu(c)
    splits = [ATT_WIDTH, 2 * ATT_WIDTH, 3 * ATT_WIDTH, 3 * ATT_WIDTH + ATT_HEADS,
              3 * ATT_WIDTH + ATT_HEADS + SSM_WIDTH]
    for l in range(DEPTH):
        ada = (cond @ w_ada[l] + b_ada[l])[:, None, :]
        sh1, sc1, gt1, sh2, sc2, gt2 = jnp.split(ada, 6, axis=-1)

        h = _rmsnorm(x, g_mix[l]) * (1.0 + sc1) + sh1
        z = h @ w_in[l]
        q, k, v, fg, u, gates = jnp.split(z, splits, axis=-1)
        log_f = jax.nn.log_sigmoid((fg + b_forget[l]).astype(jnp.float32))
        hs = (bsz, seq, ATT_HEADS, ATT_HEAD_DIM)
        y_att = _forgetting_attention(q.reshape(hs), k.reshape(hs), v.reshape(hs), log_f) @ w_proj_att[l]
        y_ssm = _s5(u, lam_re[l], lam_im[l], log_dt[l], b_re[l], b_im[l], c_re[l], c_im[l],
                    d_skip[l], w_glu[l], b_glu[l]) @ w_proj_ssm[l]
        g_att, g_ssm = jnp.split(jax.nn.sigmoid(gates + b_gate[l]), 2, axis=-1)
        mixed = (g_att * y_att + g_ssm * y_ssm) @ w_out[l]
        x = x + gt1 * mixed

        h2 = _rmsnorm(x, g_ffn[l]) * (1.0 + sc2) + sh2
        if l % 2 == 0:
            m = l // 2
            f = _swiglu(h2, w1_dense[m], w3_dense[m], w2_dense[m])
        else:
            m = l // 2
            f = _moe(h2, w_router[m], b_router[m], w1_moe[m], w3_moe[m], w2_moe[m])
        x = x + gt2 * f
    return _rmsnorm(x, g_final)
```

```python
import functools
import math

import jax
import jax.numpy as jnp
from jax import lax
from jax.experimental import pallas as pl
from jax.experimental.pallas import tpu as pltpu

F32 = jnp.float32
BF16 = jnp.bfloat16
EPS = 1e-6

ATT_HEADS = 8
ATT_HEAD_DIM = 64
SSM_GROUP = 16
SSM_STATE = 64
N_EXPERTS = 8
LANES = 128
VMEM_LIMIT = 56 * 1024 * 1024


def _params(sem):
    return pltpu.CompilerParams(dimension_semantics=sem, vmem_limit_bytes=VMEM_LIMIT)


def _dot(a, b):
    return jnp.dot(a, b, preferred_element_type=F32)


def _ada_kernel(c_ref, w_ref, b_ref, o_ref):
    cond = jax.nn.silu(c_ref[...])
    o_ref[0] = jnp.dot(cond, w_ref[0], preferred_element_type=F32,
                       precision=lax.Precision.HIGHEST) + b_ref[0]


def _ada(c, w_ada, b_ada, tn=1536):
    depth, d, n = w_ada.shape
    bsz = c.shape[0]
    return pl.pallas_call(
        _ada_kernel,
        grid=(depth, n // tn),
        in_specs=[pl.BlockSpec((bsz, d), lambda l, j: (0, 0)),
                  pl.BlockSpec((1, d, tn), lambda l, j: (l, 0, j)),
                  pl.BlockSpec((1, 1, tn), lambda l, j: (l, 0, j))],
        out_specs=pl.BlockSpec((1, bsz, tn), lambda l, j: (l, 0, j)),
        out_shape=jax.ShapeDtypeStruct((depth, bsz, n), F32),
        compiler_params=_params(("arbitrary", "arbitrary")),
        name="ada",
    )(c, w_ada, b_ada.reshape(depth, 1, n))


def _modulated_norm(x, g, shift, scale):
    y = x * lax.rsqrt(jnp.mean(x * x, axis=-1, keepdims=True) + EPS) * g
    return y * (1.0 + scale) + shift


def _inproj_kernel(x_ref, ada_ref, g_ref, w_ref, bg_ref, bf_ref,
                   qkv_ref, u_ref, gate_ref, f_ref, carry_ref, *, tm, att_w, ssm_w, d):
    j = pl.program_id(1)
    ada = ada_ref[0]
    h = _modulated_norm(x_ref[0], g_ref[...], ada[0:1], ada[1:2]).astype(BF16)

    cw = 512
    for c0 in range(0, 3 * att_w, cw):
        qkv_ref[0, :, c0:c0 + cw] = _dot(h, w_ref[:, c0:c0 + cw]).astype(BF16)
    o_u = 3 * att_w
    u_ref[0] = _dot(h, w_ref[:, o_u:o_u + ssm_w]).astype(BF16)
    o_g = o_u + ssm_w
    for c0 in range(0, 2 * d, cw):
        z = _dot(h, w_ref[:, o_g + c0:o_g + c0 + cw]) + bg_ref[:, c0:c0 + cw]
        gate_ref[0, :, c0:c0 + cw] = jax.nn.sigmoid(z).astype(BF16)

    o_f = o_g + 2 * d
    logf = jax.nn.log_sigmoid(_dot(h, w_ref[:, o_f:o_f + LANES]) + bf_ref[...])
    row = lax.broadcasted_iota(jnp.int32, (tm, tm), 0)
    col = lax.broadcasted_iota(jnp.int32, (tm, tm), 1)
    tri = (row >= col).astype(BF16)
    hi = logf.astype(BF16)
    r1 = logf - hi.astype(F32)
    mid = r1.astype(BF16)
    lo = (r1 - mid.astype(F32)).astype(BF16)
    cs = _dot(tri, hi) + _dot(tri, mid) + _dot(tri, lo)

    @pl.when(j == 0)
    def _():
        carry_ref[...] = jnp.zeros_like(carry_ref)

    fcum = cs + carry_ref[0:1, :]
    f_ref[0] = fcum
    carry_ref[...] = jnp.broadcast_to(fcum[tm - 1:tm, :], carry_ref.shape)


def _inproj(x, ada_l, g, w_all, b_gate, b_forget_pad, *, tm, att_w, ssm_w):
    bsz, seq, d = x.shape
    n_all = w_all.shape[1]
    kern = functools.partial(_inproj_kernel, tm=tm, att_w=att_w, ssm_w=ssm_w, d=d)
    return pl.pallas_call(
        kern,
        grid=(bsz, seq // tm),
        in_specs=[pl.BlockSpec((1, tm, d), lambda b, j: (b, j, 0)),
                  pl.BlockSpec((1, 6, d), lambda b, j: (b, 0, 0)),
                  pl.BlockSpec((1, d), lambda b, j: (0, 0)),
                  pl.BlockSpec((d, n_all), lambda b, j: (0, 0)),
                  pl.BlockSpec((1, 2 * d), lambda b, j: (0, 0)),
                  pl.BlockSpec((1, LANES), lambda b, j: (0, 0))],
        out_specs=[pl.BlockSpec((1, tm, 3 * att_w), lambda b, j: (b, j, 0)),
                   pl.BlockSpec((1, tm, ssm_w), lambda b, j: (b, j, 0)),
                   pl.BlockSpec((1, tm, 2 * d), lambda b, j: (b, j, 0)),
                   pl.BlockSpec((1, tm, LANES), lambda b, j: (b, j, 0))],
        out_shape=[jax.ShapeDtypeStruct((bsz, seq, 3 * att_w), BF16),
                   jax.ShapeDtypeStruct((bsz, seq, ssm_w), BF16),
                   jax.ShapeDtypeStruct((bsz, seq, 2 * d), BF16),
                   jax.ShapeDtypeStruct((bsz, seq, LANES), F32)],
        scratch_shapes=[pltpu.VMEM((8, LANES), F32)],
        compiler_params=_params(("arbitrary", "arbitrary")),
        name="inproj",
    )(x, ada_l, g, w_all, b_gate, b_forget_pad)


def _attn_kernel(q_ref, k_ref, v_ref, fc_ref, fr_ref, o_ref, *, tq, scale):
    i = pl.program_id(2)
    q2 = q_ref[0]
    lane = lax.broadcasted_iota(jnp.int32, (tq, LANES), 1)
    first = lane < ATT_HEAD_DIM
    zero = jnp.zeros_like(q2)
    q_heads = (jnp.where(first, q2, zero) * scale, jnp.where(first, zero, q2) * scale)
    fc = fc_ref[0, 0]
    row = lax.broadcasted_iota(jnp.int32, (tq, tq), 0)
    col = lax.broadcasted_iota(jnp.int32, (tq, tq), 1)
    causal = row >= col

    def step(j, carry, qh, fq, hh, masked):
        m, l, acc = carry
        start = pl.multiple_of(j * tq, tq)
        ks = k_ref[0, pl.ds(start, tq), :]
        vs = v_ref[0, pl.ds(start, tq), :]
        fk = fr_ref[0, 0, hh, pl.ds(j, 1), :]
        s = lax.dot_general(qh, ks, (((1,), (1,)), ((), ())), preferred_element_type=F32)
        r = s - fk
        if masked:
            r = jnp.where(causal, r, -jnp.inf)
        m_new = jnp.maximum(m, jnp.max(r, axis=-1, keepdims=True) + fq)
        p = jnp.exp(r + (fq - m_new))
        alpha = jnp.exp(m - m_new)
        l = alpha * l + jnp.sum(p, axis=-1, keepdims=True)
        acc = alpha * acc + _dot(p.astype(BF16), vs)
        return m_new, l, acc

    outs = []
    for hh in range(2):
        qh = q_heads[hh]
        fq = fc[:, hh:hh + 1]
        init = (jnp.full((tq, 1), -jnp.inf, F32), jnp.zeros((tq, 1), F32), jnp.zeros((tq, LANES), F32))
        carry = lax.fori_loop(0, i, functools.partial(step, qh=qh, fq=fq, hh=hh, masked=False), init)
        _, l, acc = step(i, carry, qh, fq, hh, True)
        outs.append(acc / l)
    o_ref[0] = jnp.where(first, outs[0], outs[1]).astype(BF16)


def _attention(qkv, fcol, frow, *, tq):
    bsz, seq, w3 = qkv.shape
    att_w = w3 // 3
    npair = att_w // LANES
    nkb = seq // tq
    kern = functools.partial(_attn_kernel, tq=tq, scale=ATT_HEAD_DIM ** -0.5)
    return pl.pallas_call(
        kern,
        grid=(bsz, npair, seq // tq),
        in_specs=[pl.BlockSpec((1, tq, LANES), lambda b, p, i: (b, i, p)),
                  pl.BlockSpec((1, seq, LANES), lambda b, p, i: (b, 0, npair + p)),
                  pl.BlockSpec((1, seq, LANES), lambda b, p, i: (b, 0, 2 * npair + p)),
                  pl.BlockSpec((1, 1, tq, 2), lambda b, p, i: (b, p, i, 0)),
                  pl.BlockSpec((1, 1, 2, nkb, tq), lambda b, p, i: (b, p, 0, 0, 0))],
        out_specs=pl.BlockSpec((1, tq, LANES), lambda b, p, i: (b, i, p)),
        out_shape=jax.ShapeDtypeStruct((bsz, seq, att_w), BF16),
        compiler_params=_params(("arbitrary", "arbitrary", "arbitrary")),
        name="fox_attention",
    )(qkv, qkv, qkv, fcol, frow)


def _s5_weights(lam_re, lam_im, log_dt, b_re, b_im, c_re, c_im, d_skip, chunk):
    f32 = F32
    lr, li = lam_re.astype(f32), lam_im.astype(f32)
    g, p = lr.shape
    hc = b_re.shape[-1]
    dt = jnp.exp(log_dt.astype(f32))[:, None]
    mag = jnp.exp(lr * dt)
    ang = li * dt
    a_re, a_im = mag * jnp.cos(ang), mag * jnp.sin(ang)
    n_re, n_im = a_re - 1.0, a_im
    den = lr * lr + li * li
    k_re = (n_re * lr + n_im * li) / den
    k_im = (n_im * lr - n_re * li) / den
    br, bi = b_re.astype(f32), b_im.astype(f32)
    bb_re = k_re[..., None] * br - k_im[..., None] * bi
    bb_im = k_re[..., None] * bi + k_im[..., None] * br
    cr, ci = c_re.astype(f32), c_im.astype(f32)

    tau = jnp.arange(chunk + 1, dtype=f32)
    pmag = jnp.exp((lr * dt)[..., None] * tau)
    pang = ang[..., None] * tau
    pr, pi = pmag * jnp.cos(pang), pmag * jnp.sin(pang)
    wr = cr[..., None] * pr[:, None] - ci[..., None] * pi[:, None]
    wi = cr[..., None] * pi[:, None] + ci[..., None] * pr[:, None]
    hp = lax.Precision.HIGHEST
    kern = (jnp.einsum('ghpt,gpk->gthk', wr, bb_re, precision=hp)
            - jnp.einsum('ghpt,gpk->gthk', wi, bb_im, precision=hp))
    ii = jnp.arange(chunk)
    diff = ii[None, :] - ii[:, None]
    kt = kern[:, jnp.clip(diff, 0, chunk)]
    kt = jnp.where((diff >= 0)[None, :, :, None, None], kt, 0.0)
    t_intra = kt.transpose(0, 1, 4, 2, 3).reshape(g, chunk * hc, chunk * hc)

    rev = chunk - 1 - ii
    pr_rev, pi_rev = pr[:, :, rev], pi[:, :, rev]
    s_re = pr_rev[..., None] * bb_re[:, :, None, :] - pi_rev[..., None] * bb_im[:, :, None, :]
    s_im = pr_rev[..., None] * bb_im[:, :, None, :] + pi_rev[..., None] * bb_re[:, :, None, :]
    b_state = jnp.concatenate([s_re.transpose(0, 2, 3, 1), s_im.transpose(0, 2, 3, 1)], axis=-1)
    b_state = b_state.reshape(g, chunk * hc, 2 * p)

    wr1 = wr[..., 1:].transpose(0, 2, 3, 1).reshape(g, p, chunk * hc)
    wi1 = wi[..., 1:].transpose(0, 2, 3, 1).reshape(g, p, chunk * hc)
    c_state = jnp.concatenate([wr1, -wi1], axis=1)

    ac_re, ac_im = pr[:, :, chunk], pi[:, :, chunk]
    a1 = jnp.concatenate([ac_re, ac_re], axis=-1)[:, None, :]
    a2 = jnp.concatenate([-ac_im, ac_im], axis=-1)[:, None, :]
    dvec = jnp.tile(d_skip.astype(f32), (1, chunk))[:, None, :]
    return t_intra.astype(BF16), b_state.astype(BF16), c_state.astype(BF16), a1, a2, dvec


def _s5_kernel(u_ref, t_ref, b_ref, c_ref, a1_ref, a2_ref, d_ref, y_ref, s_ref, *, bsz, nk, p):
    u = u_ref[0]
    s_ref[...] = _dot(u, b_ref[0])
    a1 = a1_ref[0]
    a2 = a2_ref[0]

    def carry_step(k, state):
        rows = pl.ds(pl.multiple_of(k * bsz, bsz), bsz)
        loc = s_ref[rows, :]
        s_ref[rows, :] = state
        return a1 * state + a2 * pltpu.roll(state, p, 1) + loc

    lax.fori_loop(0, nk, carry_step, jnp.zeros((bsz, 2 * p), F32))
    y = _dot(u, t_ref[0]) + _dot(s_ref[...].astype(BF16), c_ref[0]) + d_ref[0] * u.astype(F32)
    y_ref[0] = jax.nn.gelu(y).astype(BF16)


def _s5(u, weights, *, chunk):
    bsz, seq, w = u.shape
    t_intra, b_state, c_state, a1, a2, dvec = weights
    g = t_intra.shape[0]
    hc = w // g
    nk = seq // chunk
    m = nk * bsz
    ch = chunk * hc
    p2 = b_state.shape[-1]
    ug = u.reshape(bsz, nk, chunk, g, hc).transpose(3, 1, 0, 2, 4).reshape(g, m, ch)
    kern = functools.partial(_s5_kernel, bsz=bsz, nk=nk, p=p2 // 2)
    yg = pl.pallas_call(
        kern,
        grid=(g,),
        in_specs=[pl.BlockSpec((1, m, ch), lambda i: (i, 0, 0)),
                  pl.BlockSpec((1, ch, ch), lambda i: (i, 0, 0)),
                  pl.BlockSpec((1, ch, p2), lambda i: (i, 0, 0)),
                  pl.BlockSpec((1, p2, ch), lambda i: (i, 0, 0)),
                  pl.BlockSpec((1, 1, p2), lambda i: (i, 0, 0)),
                  pl.BlockSpec((1, 1, p2), lambda i: (i, 0, 0)),
                  pl.BlockSpec((1, 1, ch), lambda i: (i, 0, 0))],
        out_specs=pl.BlockSpec((1, m, ch), lambda i: (i, 0, 0)),
        out_shape=jax.ShapeDtypeStruct((g, m, ch), BF16),
        scratch_shapes=[pltpu.VMEM((m, p2), F32)],
        compiler_params=_params(("arbitrary",)),
        name="s5_chunked",
    )(ug, t_intra, b_state, c_state, a1, a2, dvec)
    return yg.reshape(g, nk, bsz, chunk, hc).transpose(2, 1, 3, 0, 4).reshape(bsz, seq, w)


def _route(logits):
    lane = lax.broadcasted_iota(jnp.int32, logits.shape, 1)
    valid = lane < N_EXPERTS
    neg = jnp.float32(-jnp.inf)
    lg = jnp.where(valid, logits, neg)
    m1 = jnp.max(lg, axis=-1, keepdims=True)
    i1 = jnp.min(jnp.where(lg == m1, lane, LANES), axis=-1, keepdims=True)
    lg2 = jnp.where(lane == i1, neg, lg)
    m2 = jnp.max(lg2, axis=-1, keepdims=True)
    i2 = jnp.min(jnp.where(lg2 == m2, lane, LANES), axis=-1, keepdims=True)
    e2 = jnp.exp(m2 - m1)
    den = 1.0 + e2
    return jnp.where(lane == i1, 1.0 / den, 0.0) + jnp.where(lane == i2, e2 / den, 0.0)


def _mix_kernel(att_ref, ys_ref, gate_ref, x_ref, ada_ref, wpa_ref, wglu_ref, bglu_ref, wps_ref,
                wout_ref, gffn_ref, wr_ref, br_ref, xo_ref, h2_ref, gates_ref, *, d, route):
    ada = ada_ref[0]
    ya = _dot(att_ref[0], wpa_ref[...])
    ys = ys_ref[0]
    glu = ys.astype(F32) * jax.nn.sigmoid(_dot(ys, wglu_ref[...]) + bglu_ref[...])
    yp = _dot(glu.astype(BF16), wps_ref[...])
    mixed = gate_ref[0, :, 0:d].astype(F32) * ya + gate_ref[0, :, d:2 * d].astype(F32) * yp
    x_new = x_ref[0] + ada[2:3] * _dot(mixed.astype(BF16), wout_ref[...])
    xo_ref[0] = x_new
    h2 = _modulated_norm(x_new, gffn_ref[...], ada[3:4], ada[4:5])
    h2_ref[0] = h2.astype(BF16)
    if route:
        logits = jnp.dot(h2, wr_ref[...], preferred_element_type=F32,
                         precision=lax.Precision.HIGHEST) + br_ref[...]
        gates_ref[0] = _route(logits)
    else:
        gates_ref[0] = jnp.ones(gates_ref.shape[1:], F32)


def _mix(att, ys, gates, x, ada_l, wpa, wglu, bglu, wps, wout, gffn, w_router, b_router, *, tm, route):
    bsz, seq, d = x.shape
    aw = att.shape[-1]
    sw = ys.shape[-1]
    kern = functools.partial(_mix_kernel, d=d, route=route)
    const = lambda b, j: (0, 0)
    tile = lambda b, j: (b, j, 0)
    return pl.pallas_call(
        kern,
        grid=(bsz, seq // tm),
        in_specs=[pl.BlockSpec((1, tm, aw), tile),
                  pl.BlockSpec((1, tm, sw), tile),
                  pl.BlockSpec((1, tm, 2 * d), tile),
                  pl.BlockSpec((1, tm, d), tile),
                  pl.BlockSpec((1, 6, d), lambda b, j: (b, 0, 0)),
                  pl.BlockSpec((aw, d), const),
                  pl.BlockSpec((sw, sw), const),
                  pl.BlockSpec((1, sw), const),
                  pl.BlockSpec((sw, d), const),
                  pl.BlockSpec((d, d), const),
                  pl.BlockSpec((1, d), const),
                  pl.BlockSpec((d, LANES), const),
                  pl.BlockSpec((1, LANES), const)],
        out_specs=[pl.BlockSpec((1, tm, d), tile),
                   pl.BlockSpec((1, tm, d), tile),
                   pl.BlockSpec((1, tm, LANES), tile)],
        out_shape=[jax.ShapeDtypeStruct((bsz, seq, d), F32),
                   jax.ShapeDtypeStruct((bsz, seq, d), BF16),
                   jax.ShapeDtypeStruct((bsz, seq, LANES), F32)],
        compiler_params=_params(("arbitrary", "arbitrary")),
        name="mix",
    )(att, ys, gates, x, ada_l, wpa, wglu, bglu, wps, wout, gffn, w_router, b_router)


def _ffn_kernel(h_ref, gates_ref, x_ref, ada_ref, w1_ref, w3_ref, w2_ref, gfin_ref, o_ref, acc_ref,
                *, n_exp, nf, final_norm):
    e = pl.program_id(1)
    f = pl.program_id(2)

    @pl.when((e == 0) & (f == 0))
    def _():
        acc_ref[...] = jnp.zeros_like(acc_ref)

    h = h_ref[...]
    lane = lax.broadcasted_iota(jnp.int32, gates_ref.shape, 1)
    gcol = jnp.sum(jnp.where(lane == e, gates_ref[...], 0.0), axis=-1, keepdims=True)
    a = _dot(h, w1_ref[0])
    b = _dot(h, w3_ref[0])
    act = jax.nn.silu(a) * b * gcol
    acc_ref[...] += _dot(act.astype(BF16), w2_ref[0])

    @pl.when((e == n_exp - 1) & (f == nf - 1))
    def _():
        ada = ada_ref[0]
        y = x_ref[...] + ada[5:6] * acc_ref[...]
        if final_norm:
            y = y * lax.rsqrt(jnp.mean(y * y, axis=-1, keepdims=True) + EPS) * gfin_ref[...]
        o_ref[...] = y


def _ffn(h2, gates, x, ada_l, w1, w3, w2, g_final, *, tm, cf, seq, final_norm):
    t, d = h2.shape
    n_exp, _, dff = w1.shape
    nf = dff // cf
    tiles_per_seq = seq // tm
    kern = functools.partial(_ffn_kernel, n_exp=n_exp, nf=nf, final_norm=final_norm)
    return pl.pallas_call(
        kern,
        grid=(t // tm, n_exp, nf),
        in_specs=[pl.BlockSpec((tm, d), lambda i, e, f: (i, 0)),
                  pl.BlockSpec((tm, LANES), lambda i, e, f: (i, 0)),
                  pl.BlockSpec((tm, d), lambda i, e, f: (i, 0)),
                  pl.BlockSpec((1, 6, d), lambda i, e, f: (i // tiles_per_seq, 0, 0)),
                  pl.BlockSpec((1, d, cf), lambda i, e, f: (e, 0, f)),
                  pl.BlockSpec((1, d, cf), lambda i, e, f: (e, 0, f)),
                  pl.BlockSpec((1, cf, d), lambda i, e, f: (e, f, 0)),
                  pl.BlockSpec((1, d), lambda i, e, f: (0, 0))],
        out_specs=pl.BlockSpec((tm, d), lambda i, e, f: (i, 0)),
        out_shape=jax.ShapeDtypeStruct((t, d), F32),
        scratch_shapes=[pltpu.VMEM((tm, d), F32)],
        compiler_params=_params(("arbitrary", "arbitrary", "arbitrary")),
        name="ffn",
    )(h2, gates, x, ada_l, w1, w3, w2, g_final)


def _tiles(seq):
    tm = min(512, seq)
    tq = min(256, seq)
    return tm, tq, 32, tm


def kernel(x, c, w_ada, b_ada, g_mix, w_in, b_forget, b_gate, lam_re, lam_im, log_dt, b_re, b_im,
           c_re, c_im, d_skip, w_glu, b_glu, w_proj_att, w_proj_ssm, w_out, g_ffn, w1_dense, w3_dense,
           w2_dense, w_router, b_router, w1_moe, w3_moe, w2_moe, g_final):
    bsz, seq, d = x.shape
    depth = w_ada.shape[0]
    att_w = ATT_HEADS * ATT_HEAD_DIM
    ssm_w = w_glu.shape[-1]
    dff = w1_dense.shape[-1]
    tm, tq, chunk, tf = _tiles(seq)
    cf = dff // 2

    ada = _ada(c, w_ada, b_ada).reshape(depth, bsz, 6, d)
    gfin = g_final.reshape(1, d)

    for l in range(depth):
        o1, o2, o3 = 3 * att_w, 3 * att_w + ATT_HEADS, 3 * att_w + ATT_HEADS + ssm_w
        w = w_in[l]
        w_fg = jnp.pad(w[:, o1:o2], ((0, 0), (0, LANES - ATT_HEADS)))
        w_all = jnp.concatenate([w[:, :o1], w[:, o2:o3], w[:, o3:], w_fg], axis=1).astype(BF16)
        bf_pad = jnp.pad(b_forget[l], (0, LANES - ATT_HEADS)).reshape(1, LANES)

        qkv, u, gates, fcum = _inproj(x, ada[l], g_mix[l].reshape(1, d), w_all,
                                      b_gate[l].reshape(1, 2 * d), bf_pad,
                                      tm=tm, att_w=att_w, ssm_w=ssm_w)

        fh = fcum[:, :, :ATT_HEADS]
        fcol = fh.reshape(bsz, seq, ATT_HEADS // 2, 2).transpose(0, 2, 1, 3)
        frow = fh.transpose(0, 2, 1).reshape(bsz, ATT_HEADS // 2, 2, seq // tq, tq)
        att = _attention(qkv, fcol, frow, tq=tq)

        s5w = _s5_weights(lam_re[l], lam_im[l], log_dt[l], b_re[l], b_im[l], c_re[l], c_im[l],
                          d_skip[l], chunk)
        ys = _s5(u, s5w, chunk=chunk)

        moe = l % 2 == 1
        m = l // 2
        if moe:
            wr = jnp.pad(w_router[m], ((0, 0), (0, LANES - N_EXPERTS)))
            br = jnp.pad(b_router[m], (0, LANES - N_EXPERTS)).reshape(1, LANES)
        else:
            wr = jnp.zeros((d, LANES), F32)
            br = jnp.zeros((1, LANES), F32)
        x, h2, egates = _mix(att, ys, gates, x, ada[l],
                             w_proj_att[l].astype(BF16), w_glu[l].astype(BF16),
                             b_glu[l].reshape(1, ssm_w), w_proj_ssm[l].astype(BF16),
                             w_out[l].astype(BF16), g_ffn[l].reshape(1, d), wr, br,
                             tm=tm, route=moe)

        if moe:
            w1, w3, w2 = w1_moe[m].astype(BF16), w3_moe[m].astype(BF16), w2_moe[m].astype(BF16)
        else:
            w1, w3, w2 = (w1_dense[m][None].astype(BF16), w3_dense[m][None].astype(BF16),
                          w2_dense[m][None].astype(BF16))
        x = _ffn(h2.reshape(bsz * seq, d), egates.reshape(bsz * seq, LANES), x.reshape(bsz * seq, d),
                 ada[l], w1, w3, w2, gfin, tm=tf, cf=cf, seq=seq,
                 final_norm=(l == depth - 1)).reshape(bsz, seq, d)
    return x
```

```python
import functools
import math

import jax
import jax.numpy as jnp
from jax import lax
from jax.experimental import pallas as pl
from jax.experimental.pallas import tpu as pltpu

F32 = jnp.float32
BF16 = jnp.bfloat16
EPS = 1e-6

ATT_HEADS = 8
ATT_HEAD_DIM = 64
SSM_GROUP = 16
SSM_STATE = 64
N_EXPERTS = 8
LANES = 128
SUB = 8
LOG2E = math.log2(math.e)
F_OFF = 32
VMEM_LIMIT = 56 * 1024 * 1024


def _params(sem):
    return pltpu.CompilerParams(dimension_semantics=sem, vmem_limit_bytes=VMEM_LIMIT)


def _dot(a, b):
    return jnp.dot(a, b, preferred_element_type=F32)


def _ada_kernel(c_ref, w_ref, b_ref, o_ref):
    cond = jax.nn.silu(c_ref[...])
    o_ref[0] = jnp.dot(cond, w_ref[0], preferred_element_type=F32,
                       precision=lax.Precision.HIGHEST) + b_ref[0]


def _ada(c, w_ada, b_ada, tn=1536):
    depth, d, n = w_ada.shape
    bsz = c.shape[0]
    return pl.pallas_call(
        _ada_kernel,
        grid=(depth, n // tn),
        in_specs=[pl.BlockSpec((bsz, d), lambda l, j: (0, 0)),
                  pl.BlockSpec((1, d, tn), lambda l, j: (l, 0, j)),
                  pl.BlockSpec((1, 1, tn), lambda l, j: (l, 0, j))],
        out_specs=pl.BlockSpec((1, bsz, tn), lambda l, j: (l, 0, j)),
        out_shape=jax.ShapeDtypeStruct((depth, bsz, n), F32),
        compiler_params=_params(("arbitrary", "arbitrary")),
        name="ada",
    )(c, w_ada, b_ada.reshape(depth, 1, n))


def _modulated_norm(x, g, shift, scale):
    y = x * lax.rsqrt(jnp.mean(x * x, axis=-1, keepdims=True) + EPS) * g
    return y * (1.0 + scale) + shift


def _inproj_kernel(x_ref, ada_ref, g_ref, w_ref, bg_ref, bf_ref,
                   qkv_ref, u_ref, gate_ref, qa_ref, ka_ref, carry_ref, *, tm, att_w, ssm_w, d):
    j = pl.program_id(1)
    ada = ada_ref[0]
    h = _modulated_norm(x_ref[0], g_ref[...], ada[0:1], ada[1:2]).astype(BF16)

    cw = att_w
    qkv_ref[0, :, 0:cw] = (_dot(h, w_ref[:, 0:cw]) * (ATT_HEAD_DIM ** -0.5 * LOG2E)).astype(BF16)
    for c0 in range(cw, 3 * att_w, cw):
        qkv_ref[0, :, c0:c0 + cw] = _dot(h, w_ref[:, c0:c0 + cw]).astype(BF16)
    o_u = 3 * att_w
    u_ref[0] = _dot(h, w_ref[:, o_u:o_u + ssm_w])
    o_g = o_u + ssm_w
    for c0 in range(0, 2 * d, cw):
        z = _dot(h, w_ref[:, o_g + c0:o_g + c0 + cw]) + bg_ref[:, c0:c0 + cw]
        gate_ref[0, :, c0:c0 + cw] = jax.nn.sigmoid(z).astype(BF16)

    o_f = o_g + 2 * d
    logf = jax.nn.log_sigmoid(_dot(h, w_ref[:, o_f:o_f + LANES]) + bf_ref[...])
    row = lax.broadcasted_iota(jnp.int32, (tm, tm), 0)
    col = lax.broadcasted_iota(jnp.int32, (tm, tm), 1)
    tri = (row >= col).astype(BF16)
    hi = logf.astype(BF16)
    r1 = logf - hi.astype(F32)
    mid = r1.astype(BF16)
    lo = (r1 - mid.astype(F32)).astype(BF16)
    cs = _dot(tri, hi) + _dot(tri, mid) + _dot(tri, lo)

    @pl.when(j == 0)
    def _():
        carry_ref[...] = jnp.zeros_like(carry_ref)

    fcum = cs + carry_ref[0:1, :]
    carry_ref[...] = jnp.broadcast_to(fcum[tm - 1:tm, :], carry_ref.shape)

    lane = lax.broadcasted_iota(jnp.int32, (tm, LANES), 1)
    fl = jnp.where(lane < ATT_HEADS, fcum * LOG2E, 0.0)
    p0 = fl.astype(BF16).astype(F32)
    r0 = fl - p0
    p1 = r0.astype(BF16).astype(F32)
    p2 = (r0 - p1).astype(BF16).astype(F32)
    comb = p0 + pltpu.roll(p1, ATT_HEADS, 1) + pltpu.roll(p2, 2 * ATT_HEADS, 1)
    sel_lo = (lane < 3 * ATT_HEADS).astype(F32)
    sel_hi = ((lane >= F_OFF) & (lane < F_OFF + 3 * ATT_HEADS)).astype(F32)
    qa_ref[0] = (pltpu.roll(comb, F_OFF, 1) + sel_lo).astype(BF16)
    ka_ref[0] = (sel_hi - comb).astype(BF16)


def _inproj(x, ada_l, g, w_all, b_gate, b_forget_pad, *, tm, att_w, ssm_w):
    bsz, seq, d = x.shape
    n_all = w_all.shape[1]
    kern = functools.partial(_inproj_kernel, tm=tm, att_w=att_w, ssm_w=ssm_w, d=d)
    return pl.pallas_call(
        kern,
        grid=(bsz, seq // tm),
        in_specs=[pl.BlockSpec((1, tm, d), lambda b, j: (b, j, 0)),
                  pl.BlockSpec((1, 6, d), lambda b, j: (b, 0, 0)),
                  pl.BlockSpec((1, d), lambda b, j: (0, 0)),
                  pl.BlockSpec((d, n_all), lambda b, j: (0, 0)),
                  pl.BlockSpec((1, 2 * d), lambda b, j: (0, 0)),
                  pl.BlockSpec((1, LANES), lambda b, j: (0, 0))],
        out_specs=[pl.BlockSpec((1, tm, 3 * att_w), lambda b, j: (b, j, 0)),
                   pl.BlockSpec((1, tm, ssm_w), lambda b, j: (b, j, 0)),
                   pl.BlockSpec((1, tm, 2 * d), lambda b, j: (b, j, 0)),
                   pl.BlockSpec((1, tm, LANES), lambda b, j: (b, j, 0)),
                   pl.BlockSpec((1, tm, LANES), lambda b, j: (b, j, 0))],
        out_shape=[jax.ShapeDtypeStruct((bsz, seq, 3 * att_w), BF16),
                   jax.ShapeDtypeStruct((bsz, seq, ssm_w), F32),
                   jax.ShapeDtypeStruct((bsz, seq, 2 * d), BF16),
                   jax.ShapeDtypeStruct((bsz, seq, LANES), BF16),
                   jax.ShapeDtypeStruct((bsz, seq, LANES), BF16)],
        scratch_shapes=[pltpu.VMEM((8, LANES), F32)],
        compiler_params=_params(("arbitrary", "arbitrary")),
        name="inproj",
    )(x, ada_l, g, w_all, b_gate, b_forget_pad)


def _attn_kernel(q_ref, qa_ref, k_ref, ka_ref, v_ref, o_ref, *, tq):
    hp = pl.program_id(1)
    i = pl.program_id(2)
    half = tq // 2
    lane = lax.broadcasted_iota(jnp.int32, (tq, LANES), 1)
    first = lane < ATT_HEAD_DIM
    q2 = q_ref[0]
    qa = qa_ref[0]
    zero = jnp.zeros_like(q2)
    heads = []
    for hh in range(2):
        qsel = jnp.where(first, q2, zero) if hh == 0 else jnp.where(first, zero, q2)
        asel = jnp.where((lane & (ATT_HEADS - 1)) == 2 * hp + hh, qa, zero)
        heads.append(jnp.concatenate([qsel, asel], axis=1))

    def kv(start, size):
        kj = jnp.concatenate([k_ref[0, pl.ds(start, size), :], ka_ref[0, pl.ds(start, size), :]], axis=1)
        v2 = v_ref[0, pl.ds(start, size), :]
        one = jnp.ones_like(v2)
        lk = lax.broadcasted_iota(jnp.int32, v2.shape, 1) < ATT_HEAD_DIM
        return kj, (jnp.where(lk, v2, one), jnp.where(lk, one, v2))

    def update(qh, kj, vh, m, acc, mask):
        s = lax.dot_general(qh, kj, (((1,), (1,)), ((), ())), preferred_element_type=F32)
        if mask is not None:
            s = jnp.where(mask, s, -jnp.inf)
        m_new = jnp.maximum(m, jnp.max(s, axis=-1, keepdims=True))
        p = jnp.exp2(s - m_new).astype(BF16)
        acc = jnp.exp2(m - m_new) * acc + _dot(p, vh)
        return m_new, acc

    def body(j, carry):
        kj, vs = kv(pl.multiple_of(j * tq, tq), tq)
        out = []
        for hh in range(2):
            out.extend(update(heads[hh], kj, vs[hh], carry[2 * hh], carry[2 * hh + 1], None))
        return tuple(out)

    init = []
    for hh in range(2):
        init += [jnp.full((tq, 1), -jnp.inf, F32), jnp.zeros((tq, LANES), F32)]
    carry = lax.fori_loop(0, i, body, tuple(init))

    base = pl.multiple_of(i * tq, tq)
    r_top = lax.broadcasted_iota(jnp.int32, (half, half), 0)
    c_top = lax.broadcasted_iota(jnp.int32, (half, half), 1)
    r_bot = lax.broadcasted_iota(jnp.int32, (half, tq), 0)
    c_bot = lax.broadcasted_iota(jnp.int32, (half, tq), 1)
    k_top, v_top = kv(base, half)
    k_all, v_all = kv(base, tq)
    outs = []
    for hh in range(2):
        m, acc = carry[2 * hh], carry[2 * hh + 1]
        qh = heads[hh]
        _, acc_t = update(qh[:half], k_top, v_top[hh], m[:half], acc[:half], r_top >= c_top)
        _, acc_b = update(qh[half:], k_all, v_all[hh], m[half:], acc[half:], r_bot + half >= c_bot)
        acc = jnp.concatenate([acc_t, acc_b], axis=0)
        outs.append(acc / pltpu.roll(acc, ATT_HEAD_DIM, 1))
    o_ref[0] = jnp.where(first, outs[0], outs[1]).astype(BF16)


def _attention(qkv, qaug, kaug, *, tq):
    bsz, seq, w3 = qkv.shape
    att_w = w3 // 3
    npair = att_w // LANES
    kern = functools.partial(_attn_kernel, tq=tq)
    return pl.pallas_call(
        kern,
        grid=(bsz, npair, seq // tq),
        in_specs=[pl.BlockSpec((1, tq, LANES), lambda b, p, i: (b, i, p)),
                  pl.BlockSpec((1, tq, LANES), lambda b, p, i: (b, i, 0)),
                  pl.BlockSpec((1, seq, LANES), lambda b, p, i: (b, 0, npair + p)),
                  pl.BlockSpec((1, seq, LANES), lambda b, p, i: (b, 0, 0)),
                  pl.BlockSpec((1, seq, LANES), lambda b, p, i: (b, 0, 2 * npair + p))],
        out_specs=pl.BlockSpec((1, tq, LANES), lambda b, p, i: (b, i, p)),
        out_shape=jax.ShapeDtypeStruct((bsz, seq, att_w), BF16),
        compiler_params=_params(("arbitrary", "arbitrary", "arbitrary")),
        name="fox_attention",
    )(qkv, qaug, qkv, kaug, qkv)


def _s5_weights(lam_re, lam_im, log_dt, b_re, b_im, c_re, c_im, d_skip):
    f32 = F32
    chunk = SUB
    lr, li = lam_re.astype(f32), lam_im.astype(f32)
    g, p = lr.shape
    hc = b_re.shape[-1]
    gl = LANES // hc
    nt = g // gl
    dt = jnp.exp(log_dt.astype(f32))[:, None]
    mag = jnp.exp(lr * dt)
    ang = li * dt
    a_re, a_im = mag * jnp.cos(ang), mag * jnp.sin(ang)
    n_re, n_im = a_re - 1.0, a_im
    den = lr * lr + li * li
    k_re = (n_re * lr + n_im * li) / den
    k_im = (n_im * lr - n_re * li) / den
    br, bi = b_re.astype(f32), b_im.astype(f32)
    bb_re = k_re[..., None] * br - k_im[..., None] * bi
    bb_im = k_re[..., None] * bi + k_im[..., None] * br
    cr, ci = c_re.astype(f32), c_im.astype(f32)

    tau = jnp.arange(chunk + 1, dtype=f32)
    pmag = jnp.exp((lr * dt)[..., None] * tau)
    pang = ang[..., None] * tau
    pr, pi = pmag * jnp.cos(pang), pmag * jnp.sin(pang)
    wr = cr[..., None] * pr[:, None] - ci[..., None] * pi[:, None]
    wi = cr[..., None] * pi[:, None] + ci[..., None] * pr[:, None]
    hp = lax.Precision.HIGHEST
    kern = (jnp.einsum('ghpt,gpk->gthk', wr, bb_re, precision=hp)
            - jnp.einsum('ghpt,gpk->gthk', wi, bb_im, precision=hp))
    ii = jnp.arange(chunk)
    diff = ii[None, :] - ii[:, None]
    kt = kern[:, jnp.clip(diff, 0, chunk)]
    kt = jnp.where((diff >= 0)[None, :, :, None, None], kt, 0.0)
    eye = jnp.eye(gl, dtype=f32)
    kt = kt.reshape(nt, gl, chunk, chunk, hc, hc)
    t_intra = jnp.einsum('tajihk,ab->tjakibh', kt, eye).reshape(nt, chunk * LANES, chunk * LANES)

    rev = chunk - 1 - ii
    pr_rev, pi_rev = pr[:, :, rev], pi[:, :, rev]
    s_re = pr_rev[..., None] * bb_re[:, :, None, :] - pi_rev[..., None] * bb_im[:, :, None, :]
    s_im = pr_rev[..., None] * bb_im[:, :, None, :] + pi_rev[..., None] * bb_re[:, :, None, :]
    st = jnp.stack([s_re, s_im], axis=1).reshape(nt, gl, 2, p, chunk, hc)
    b_state = jnp.einsum('tacpjk,ab->tjakcbp', st, eye).reshape(nt, chunk * LANES, 2 * gl * p)

    w1 = jnp.stack([wr[..., 1:], -wi[..., 1:]], axis=1).reshape(nt, gl, 2, hc, p, chunk)
    c_state = jnp.einsum('tachpi,ab->tcapibh', w1, eye).reshape(nt, 2 * gl * p, chunk * LANES)

    ac_re = pr[:, :, chunk].reshape(nt, 1, gl * p)
    ac_im = pi[:, :, chunk].reshape(nt, 1, gl * p)
    a1 = jnp.concatenate([ac_re, ac_re], axis=-1)
    a2 = jnp.concatenate([-ac_im, ac_im], axis=-1)
    dvec = d_skip.astype(f32).reshape(nt, 1, LANES)
    return t_intra.astype(BF16), b_state.astype(BF16), c_state.astype(BF16), a1, a2, dvec


def _s5_kernel(u_ref, t_ref, b_ref, c_ref, a1_ref, a2_ref, d_ref, y_ref, uf_ref, s_ref, *, bb, nk):
    nc = s_ref.shape[0]
    for b in range(bb):
        for j in range(SUB):
            uf_ref[b * nk:(b + 1) * nk, j * LANES:(j + 1) * LANES] = (
                u_ref[b, pl.ds(j, nk, stride=SUB), :].astype(BF16))
    for b in range(bb):
        rows = slice(b * nk, (b + 1) * nk)
        s_loc = _dot(uf_ref[rows, :], b_ref[0])
        for c in range(nc):
            s_ref[c, rows, :] = s_loc[:, c * LANES:(c + 1) * LANES]
    a1 = a1_ref[0]
    a2 = a2_ref[0]

    def carry_step(k, state):
        rows = pl.ds(k, bb, stride=nk)
        loc = jnp.concatenate([s_ref[c, rows, :] for c in range(nc)], axis=1)
        for c in range(nc):
            s_ref[c, rows, :] = state[:, c * LANES:(c + 1) * LANES]
        return a1 * state + a2 * pltpu.roll(state, nc * LANES // 2, 1) + loc

    lax.fori_loop(0, nk, carry_step, jnp.zeros((bb, nc * LANES), F32))
    d = d_ref[0]
    for b in range(bb):
        rows = slice(b * nk, (b + 1) * nk)
        s_prev = jnp.concatenate([s_ref[c, rows, :] for c in range(nc)], axis=1).astype(BF16)
        y = _dot(uf_ref[rows, :], t_ref[0]) + _dot(s_prev, c_ref[0])
        for i in range(SUB):
            yi = y[:, i * LANES:(i + 1) * LANES] + d * u_ref[b, pl.ds(i, nk, stride=SUB), :]
            y_ref[b, pl.ds(i, nk, stride=SUB), :] = jax.nn.gelu(yi)


def _s5(u, weights, *, bb):
    bsz, seq, w = u.shape
    t_intra, b_state, c_state, a1, a2, dvec = weights
    nt = t_intra.shape[0]
    nk = seq // SUB
    ch = SUB * LANES
    ns = b_state.shape[-1]
    kern = functools.partial(_s5_kernel, bb=bb, nk=nk)
    return pl.pallas_call(
        kern,
        grid=(nt, bsz // bb),
        in_specs=[pl.BlockSpec((bb, seq, LANES), lambda t, i: (i, 0, t)),
                  pl.BlockSpec((1, ch, ch), lambda t, i: (t, 0, 0)),
                  pl.BlockSpec((1, ch, ns), lambda t, i: (t, 0, 0)),
                  pl.BlockSpec((1, ns, ch), lambda t, i: (t, 0, 0)),
                  pl.BlockSpec((1, 1, ns), lambda t, i: (t, 0, 0)),
                  pl.BlockSpec((1, 1, ns), lambda t, i: (t, 0, 0)),
                  pl.BlockSpec((1, 1, LANES), lambda t, i: (t, 0, 0))],
        out_specs=pl.BlockSpec((bb, seq, LANES), lambda t, i: (i, 0, t)),
        out_shape=jax.ShapeDtypeStruct((bsz, seq, w), F32),
        scratch_shapes=[pltpu.VMEM((bb * nk, ch), BF16), pltpu.VMEM((ns // LANES, bb * nk, LANES), F32)],
        compiler_params=_params(("arbitrary", "arbitrary")),
        name="s5_chunked",
    )(u, t_intra, b_state, c_state, a1, a2, dvec)


def _route(logits):
    lane = lax.broadcasted_iota(jnp.int32, logits.shape, 1)
    valid = lane < N_EXPERTS
    neg = jnp.float32(-jnp.inf)
    lg = jnp.where(valid, logits, neg)
    m1 = jnp.max(lg, axis=-1, keepdims=True)
    i1 = jnp.min(jnp.where(lg == m1, lane, LANES), axis=-1, keepdims=True)
    lg2 = jnp.where(lane == i1, neg, lg)
    m2 = jnp.max(lg2, axis=-1, keepdims=True)
    i2 = jnp.min(jnp.where(lg2 == m2, lane, LANES), axis=-1, keepdims=True)
    e2 = jnp.exp(m2 - m1)
    den = 1.0 + e2
    return jnp.where(lane == i1, 1.0 / den, 0.0) + jnp.where(lane == i2, e2 / den, 0.0)


def _mix_kernel(att_ref, ys_ref, gate_ref, x_ref, ada_ref, wpa_ref, wglu_ref, bglu_ref, wps_ref,
                wout_ref, gffn_ref, wrh_ref, wrl_ref, br_ref, xo_ref, h2_ref, gates_ref, *, d, route):
    ada = ada_ref[0]
    ya = _dot(att_ref[0], wpa_ref[...])
    ys = ys_ref[0]
    glu = ys * jax.nn.sigmoid(_dot(ys.astype(BF16), wglu_ref[...]) + bglu_ref[...])
    yp = _dot(glu.astype(BF16), wps_ref[...])
    mixed = gate_ref[0, :, 0:d].astype(F32) * ya + gate_ref[0, :, d:2 * d].astype(F32) * yp
    x_new = x_ref[0] + ada[2:3] * _dot(mixed.astype(BF16), wout_ref[...])
    xo_ref[0] = x_new
    h2 = _modulated_norm(x_new, gffn_ref[...], ada[3:4], ada[4:5])
    h2b = h2.astype(BF16)
    h2_ref[0] = h2b
    if route:
        h2l = (h2 - h2b.astype(F32)).astype(BF16)
        logits = (_dot(h2b, wrh_ref[...]) + _dot(h2b, wrl_ref[...]) + _dot(h2l, wrh_ref[...])) + br_ref[...]
        gates_ref[0] = _route(logits)
    else:
        gates_ref[0] = jnp.ones(gates_ref.shape[1:], F32)


def _mix(att, ys, gates, x, ada_l, wpa, wglu, bglu, wps, wout, gffn, wr_hi, wr_lo, b_router, *, tm, route):
    bsz, seq, d = x.shape
    aw = att.shape[-1]
    sw = ys.shape[-1]
    kern = functools.partial(_mix_kernel, d=d, route=route)
    const = lambda b, j: (0, 0)
    tile = lambda b, j: (b, j, 0)
    return pl.pallas_call(
        kern,
        grid=(bsz, seq // tm),
        in_specs=[pl.BlockSpec((1, tm, aw), tile),
                  pl.BlockSpec((1, tm, sw), tile),
                  pl.BlockSpec((1, tm, 2 * d), tile),
                  pl.BlockSpec((1, tm, d), tile),
                  pl.BlockSpec((1, 6, d), lambda b, j: (b, 0, 0)),
                  pl.BlockSpec((aw, d), const),
                  pl.BlockSpec((sw, sw), const),
                  pl.BlockSpec((1, sw), const),
                  pl.BlockSpec((sw, d), const),
                  pl.BlockSpec((d, d), const),
                  pl.BlockSpec((1, d), const),
                  pl.BlockSpec((d, LANES), const),
                  pl.BlockSpec((d, LANES), const),
                  pl.BlockSpec((1, LANES), const)],
        out_specs=[pl.BlockSpec((1, tm, d), tile),
                   pl.BlockSpec((1, tm, d), tile),
                   pl.BlockSpec((1, tm, LANES), tile)],
        out_shape=[jax.ShapeDtypeStruct((bsz, seq, d), F32),
                   jax.ShapeDtypeStruct((bsz, seq, d), BF16),
                   jax.ShapeDtypeStruct((bsz, seq, LANES), F32)],
        compiler_params=_params(("arbitrary", "arbitrary")),
        name="mix",
    )(att, ys, gates, x, ada_l, wpa, wglu, bglu, wps, wout, gffn, wr_hi, wr_lo, b_router)


def _ffn_kernel(h_ref, gates_ref, x_ref, ada_ref, w1_ref, w3_ref, w2_ref, gfin_ref, o_ref, acc_ref,
                *, n_exp, nf, final_norm):
    e = pl.program_id(1)
    f = pl.program_id(2)

    @pl.when((e == 0) & (f == 0))
    def _():
        acc_ref[...] = jnp.zeros_like(acc_ref)

    h = h_ref[...]
    lane = lax.broadcasted_iota(jnp.int32, gates_ref.shape, 1)
    gcol = jnp.sum(jnp.where(lane == e, gates_ref[...], 0.0), axis=-1, keepdims=True)
    a = _dot(h, w1_ref[0])
    b = _dot(h, w3_ref[0])
    act = jax.nn.silu(a) * b * gcol
    acc_ref[...] += _dot(act.astype(BF16), w2_ref[0])

    @pl.when((e == n_exp - 1) & (f == nf - 1))
    def _():
        ada = ada_ref[0]
        y = x_ref[...] + ada[5:6] * acc_ref[...]
        if final_norm:
            y = y * lax.rsqrt(jnp.mean(y * y, axis=-1, keepdims=True) + EPS) * gfin_ref[...]
        o_ref[...] = y


def _ffn(h2, gates, x, ada_l, w1, w3, w2, g_final, *, tm, cf, seq, final_norm):
    t, d = h2.shape
    n_exp, _, dff = w1.shape
    nf = dff // cf
    tiles_per_seq = seq // tm
    kern = functools.partial(_ffn_kernel, n_exp=n_exp, nf=nf, final_norm=final_norm)
    return pl.pallas_call(
        kern,
        grid=(t // tm, n_exp, nf),
        in_specs=[pl.BlockSpec((tm, d), lambda i, e, f: (i, 0)),
                  pl.BlockSpec((tm, LANES), lambda i, e, f: (i, 0)),
                  pl.BlockSpec((tm, d), lambda i, e, f: (i, 0)),
                  pl.BlockSpec((1, 6, d), lambda i, e, f: (i // tiles_per_seq, 0, 0)),
                  pl.BlockSpec((1, d, cf), lambda i, e, f: (e, 0, f)),
                  pl.BlockSpec((1, d, cf), lambda i, e, f: (e, 0, f)),
                  pl.BlockSpec((1, cf, d), lambda i, e, f: (e, f, 0)),
                  pl.BlockSpec((1, d), lambda i, e, f: (0, 0))],
        out_specs=pl.BlockSpec((tm, d), lambda i, e, f: (i, 0)),
        out_shape=jax.ShapeDtypeStruct((t, d), F32),
        scratch_shapes=[pltpu.VMEM((tm, d), F32)],
        compiler_params=_params(("arbitrary", "arbitrary", "arbitrary")),
        name="ffn",
    )(h2, gates, x, ada_l, w1, w3, w2, g_final)


def _tiles(seq, bsz):
    tm = min(512, seq)
    tq = min(1024, seq // 2)
    return tm, tq, min(4, bsz), tm


def kernel(x, c, w_ada, b_ada, g_mix, w_in, b_forget, b_gate, lam_re, lam_im, log_dt, b_re, b_im,
           c_re, c_im, d_skip, w_glu, b_glu, w_proj_att, w_proj_ssm, w_out, g_ffn, w1_dense, w3_dense,
           w2_dense, w_router, b_router, w1_moe, w3_moe, w2_moe, g_final):
    bsz, seq, d = x.shape
    depth = w_ada.shape[0]
    att_w = ATT_HEADS * ATT_HEAD_DIM
    ssm_w = w_glu.shape[-1]
    dff = w1_dense.shape[-1]
    tm, tq, bb, tf = _tiles(seq, bsz)
    cf = dff // 2

    ada = _ada(c, w_ada, b_ada).reshape(depth, bsz, 6, d)
    gfin = g_final.reshape(1, d)

    for l in range(depth):
        o1, o2, o3 = 3 * att_w, 3 * att_w + ATT_HEADS, 3 * att_w + ATT_HEADS + ssm_w
        w = w_in[l]
        w_fg = jnp.pad(w[:, o1:o2], ((0, 0), (0, LANES - ATT_HEADS)))
        w_all = jnp.concatenate([w[:, :o1], w[:, o2:o3], w[:, o3:], w_fg], axis=1).astype(BF16)
        bf_pad = jnp.pad(b_forget[l], (0, LANES - ATT_HEADS)).reshape(1, LANES)

        qkv, u, gates, qaug, kaug = _inproj(x, ada[l], g_mix[l].reshape(1, d), w_all,
                                      b_gate[l].reshape(1, 2 * d), bf_pad,
                                      tm=tm, att_w=att_w, ssm_w=ssm_w)

        att = _attention(qkv, qaug, kaug, tq=tq)

        s5w = _s5_weights(lam_re[l], lam_im[l], log_dt[l], b_re[l], b_im[l], c_re[l], c_im[l], d_skip[l])
        ys = _s5(u, s5w, bb=bb)

        moe = l % 2 == 1
        m = l // 2
        if moe:
            wr = jnp.pad(w_router[m], ((0, 0), (0, LANES - N_EXPERTS)))
            br = jnp.pad(b_router[m], (0, LANES - N_EXPERTS)).reshape(1, LANES)
        else:
            wr = jnp.zeros((d, LANES), F32)
            br = jnp.zeros((1, LANES), F32)
        wr_hi = wr.astype(BF16)
        wr_lo = (wr - wr_hi.astype(F32)).astype(BF16)
        x, h2, egates = _mix(att, ys, gates, x, ada[l],
                             w_proj_att[l].astype(BF16), w_glu[l].astype(BF16),
                             b_glu[l].reshape(1, ssm_w), w_proj_ssm[l].astype(BF16),
                             w_out[l].astype(BF16), g_ffn[l].reshape(1, d), wr_hi, wr_lo, br,
                             tm=tm, route=moe)

        if moe:
            w1, w3, w2 = w1_moe[m].astype(BF16), w3_moe[m].astype(BF16), w2_moe[m].astype(BF16)
        else:
            w1, w3, w2 = (w1_dense[m][None].astype(BF16), w3_dense[m][None].astype(BF16),
                          w2_dense[m][None].astype(BF16))
        x = _ffn(h2.reshape(bsz * seq, d), egates.reshape(bsz * seq, LANES), x.reshape(bsz * seq, d),
                 ada[l], w1, w3, w2, gfin, tm=tf, cf=cf, seq=seq,
                 final_norm=(l == depth - 1)).reshape(bsz, seq, d)
    return x
```

```python
import functools
import math

import jax
import jax.numpy as jnp
from jax import lax
from jax.experimental import pallas as pl
from jax.experimental.pallas import tpu as pltpu

F32 = jnp.float32
BF16 = jnp.bfloat16
EPS = 1e-6

ATT_HEADS = 8
ATT_HEAD_DIM = 64
SSM_GROUP = 16
SSM_STATE = 64
N_EXPERTS = 8
LANES = 128
SUB = 8
LOG2E = math.log2(math.e)
F_OFF = 32
VMEM_LIMIT = 56 * 1024 * 1024


def _params(sem):
    return pltpu.CompilerParams(dimension_semantics=sem, vmem_limit_bytes=VMEM_LIMIT)


def _dot(a, b):
    return jnp.dot(a, b, preferred_element_type=F32)


def _ada_kernel(c_ref, w_ref, b_ref, o_ref):
    cond = jax.nn.silu(c_ref[...])
    o_ref[0] = jnp.dot(cond, w_ref[0], preferred_element_type=F32,
                       precision=lax.Precision.HIGHEST) + b_ref[0]


def _ada(c, w_ada, b_ada, tn=1536):
    depth, d, n = w_ada.shape
    bsz = c.shape[0]
    return pl.pallas_call(
        _ada_kernel,
        grid=(depth, n // tn),
        in_specs=[pl.BlockSpec((bsz, d), lambda l, j: (0, 0)),
                  pl.BlockSpec((1, d, tn), lambda l, j: (l, 0, j)),
                  pl.BlockSpec((1, 1, tn), lambda l, j: (l, 0, j))],
        out_specs=pl.BlockSpec((1, bsz, tn), lambda l, j: (l, 0, j)),
        out_shape=jax.ShapeDtypeStruct((depth, bsz, n), F32),
        compiler_params=_params(("arbitrary", "arbitrary")),
        name="ada",
    )(c, w_ada, b_ada.reshape(depth, 1, n))


def _modulated_norm(x, g, shift, scale):
    y = x * lax.rsqrt(jnp.mean(x * x, axis=-1, keepdims=True) + EPS) * g
    return y * (1.0 + scale) + shift


def _inproj_kernel(x_ref, ada_ref, g_ref, w_ref, bg_ref, bf_ref,
                   qkv_ref, u_ref, gate_ref, qa_ref, ka_ref, carry_ref, *, tm, att_w, ssm_w, d):
    j = pl.program_id(1)
    ada = ada_ref[0]
    h = _modulated_norm(x_ref[0], g_ref[...], ada[0:1], ada[1:2]).astype(BF16)

    cw = att_w
    qkv_ref[0, :, 0:cw] = (_dot(h, w_ref[:, 0:cw]) * (ATT_HEAD_DIM ** -0.5 * LOG2E)).astype(BF16)
    for c0 in range(cw, 3 * att_w, cw):
        qkv_ref[0, :, c0:c0 + cw] = _dot(h, w_ref[:, c0:c0 + cw]).astype(BF16)
    o_u = 3 * att_w
    u_ref[0] = _dot(h, w_ref[:, o_u:o_u + ssm_w])
    o_g = o_u + ssm_w
    for c0 in range(0, 2 * d, cw):
        z = _dot(h, w_ref[:, o_g + c0:o_g + c0 + cw]) + bg_ref[:, c0:c0 + cw]
        gate_ref[0, :, c0:c0 + cw] = jax.nn.sigmoid(z).astype(BF16)

    o_f = o_g + 2 * d
    logf = jax.nn.log_sigmoid(_dot(h, w_ref[:, o_f:o_f + LANES]) + bf_ref[...])
    row = lax.broadcasted_iota(jnp.int32, (tm, tm), 0)
    col = lax.broadcasted_iota(jnp.int32, (tm, tm), 1)
    tri = (row >= col).astype(BF16)
    hi = logf.astype(BF16)
    r1 = logf - hi.astype(F32)
    mid = r1.astype(BF16)
    lo = (r1 - mid.astype(F32)).astype(BF16)
    cs = _dot(tri, hi) + _dot(tri, mid) + _dot(tri, lo)

    @pl.when(j == 0)
    def _():
        carry_ref[...] = jnp.zeros_like(carry_ref)

    fcum = cs + carry_ref[0:1, :]
    carry_ref[...] = jnp.broadcast_to(fcum[tm - 1:tm, :], carry_ref.shape)

    lane = lax.broadcasted_iota(jnp.int32, (tm, LANES), 1)
    fl = jnp.where(lane < ATT_HEADS, fcum * LOG2E, 0.0)
    p0 = fl.astype(BF16).astype(F32)
    r0 = fl - p0
    p1 = r0.astype(BF16).astype(F32)
    p2 = (r0 - p1).astype(BF16).astype(F32)
    comb = p0 + pltpu.roll(p1, ATT_HEADS, 1) + pltpu.roll(p2, 2 * ATT_HEADS, 1)
    sel_lo = (lane < 3 * ATT_HEADS).astype(F32)
    sel_hi = ((lane >= F_OFF) & (lane < F_OFF + 3 * ATT_HEADS)).astype(F32)
    qa_ref[0] = (pltpu.roll(comb, F_OFF, 1) + sel_lo).astype(BF16)
    ka_ref[0] = (sel_hi - comb).astype(BF16)


def _inproj(x, ada_l, g, w_all, b_gate, b_forget_pad, *, tm, att_w, ssm_w):
    bsz, seq, d = x.shape
    n_all = w_all.shape[1]
    kern = functools.partial(_inproj_kernel, tm=tm, att_w=att_w, ssm_w=ssm_w, d=d)
    return pl.pallas_call(
        kern,
        grid=(bsz, seq // tm),
        in_specs=[pl.BlockSpec((1, tm, d), lambda b, j: (b, j, 0)),
                  pl.BlockSpec((1, 6, d), lambda b, j: (b, 0, 0)),
                  pl.BlockSpec((1, d), lambda b, j: (0, 0)),
                  pl.BlockSpec((d, n_all), lambda b, j: (0, 0)),
                  pl.BlockSpec((1, 2 * d), lambda b, j: (0, 0)),
                  pl.BlockSpec((1, LANES), lambda b, j: (0, 0))],
        out_specs=[pl.BlockSpec((1, tm, 3 * att_w), lambda b, j: (b, j, 0)),
                   pl.BlockSpec((1, tm, ssm_w), lambda b, j: (b, j, 0)),
                   pl.BlockSpec((1, tm, 2 * d), lambda b, j: (b, j, 0)),
                   pl.BlockSpec((1, tm, LANES), lambda b, j: (b, j, 0)),
                   pl.BlockSpec((1, tm, LANES), lambda b, j: (b, j, 0))],
        out_shape=[jax.ShapeDtypeStruct((bsz, seq, 3 * att_w), BF16),
                   jax.ShapeDtypeStruct((bsz, seq, ssm_w), F32),
                   jax.ShapeDtypeStruct((bsz, seq, 2 * d), BF16),
                   jax.ShapeDtypeStruct((bsz, seq, LANES), BF16),
                   jax.ShapeDtypeStruct((bsz, seq, LANES), BF16)],
        scratch_shapes=[pltpu.VMEM((8, LANES), F32)],
        compiler_params=_params(("arbitrary", "arbitrary")),
        name="inproj",
    )(x, ada_l, g, w_all, b_gate, b_forget_pad)


def _attn_kernel(q_ref, qa_ref, k_ref, ka_ref, v_ref, o_ref, *, tq):
    hp = pl.program_id(1)
    i = pl.program_id(2)
    half = tq // 2
    lane = lax.broadcasted_iota(jnp.int32, (tq, LANES), 1)
    first = lane < ATT_HEAD_DIM
    q2 = q_ref[0]
    qa = qa_ref[0]
    zero = jnp.zeros_like(q2)
    heads = []
    for hh in range(2):
        qsel = jnp.where(first, q2, zero) if hh == 0 else jnp.where(first, zero, q2)
        asel = jnp.where((lane & (ATT_HEADS - 1)) == 2 * hp + hh, qa, zero)
        heads.append(jnp.concatenate([qsel, asel], axis=1))

    def kv(start, size):
        kj = jnp.concatenate([k_ref[0, pl.ds(start, size), :], ka_ref[0, pl.ds(start, size), :]], axis=1)
        v2 = v_ref[0, pl.ds(start, size), :]
        one = jnp.ones_like(v2)
        lk = lax.broadcasted_iota(jnp.int32, v2.shape, 1) < ATT_HEAD_DIM
        return kj, (jnp.where(lk, v2, one), jnp.where(lk, one, v2))

    def update(qh, kj, vh, m, acc, mask):
        s = lax.dot_general(qh, kj, (((1,), (1,)), ((), ())), preferred_element_type=F32)
        if mask is not None:
            s = jnp.where(mask, s, -jnp.inf)
        m_new = jnp.maximum(m, jnp.max(s, axis=-1, keepdims=True))
        p = jnp.exp2(s - m_new).astype(BF16)
        acc = jnp.exp2(m - m_new) * acc + _dot(p, vh)
        return m_new, acc

    def body(j, carry):
        kj, vs = kv(pl.multiple_of(j * tq, tq), tq)
        out = []
        for hh in range(2):
            out.extend(update(heads[hh], kj, vs[hh], carry[2 * hh], carry[2 * hh + 1], None))
        return tuple(out)

    init = []
    for hh in range(2):
        init += [jnp.full((tq, 1), -jnp.inf, F32), jnp.zeros((tq, LANES), F32)]
    carry = lax.fori_loop(0, i, body, tuple(init))

    base = pl.multiple_of(i * tq, tq)
    r_top = lax.broadcasted_iota(jnp.int32, (half, half), 0)
    c_top = lax.broadcasted_iota(jnp.int32, (half, half), 1)
    r_bot = lax.broadcasted_iota(jnp.int32, (half, tq), 0)
    c_bot = lax.broadcasted_iota(jnp.int32, (half, tq), 1)
    k_top, v_top = kv(base, half)
    k_all, v_all = kv(base, tq)
    outs = []
    for hh in range(2):
        m, acc = carry[2 * hh], carry[2 * hh + 1]
        qh = heads[hh]
        _, acc_t = update(qh[:half], k_top, v_top[hh], m[:half], acc[:half], r_top >= c_top)
        _, acc_b = update(qh[half:], k_all, v_all[hh], m[half:], acc[half:], r_bot + half >= c_bot)
        acc = jnp.concatenate([acc_t, acc_b], axis=0)
        outs.append(acc / pltpu.roll(acc, ATT_HEAD_DIM, 1))
    o_ref[0] = jnp.where(first, outs[0], outs[1]).astype(BF16)


def _attention(qkv, qaug, kaug, *, tq):
    bsz, seq, w3 = qkv.shape
    att_w = w3 // 3
    npair = att_w // LANES
    kern = functools.partial(_attn_kernel, tq=tq)
    return pl.pallas_call(
        kern,
        grid=(bsz, npair, seq // tq),
        in_specs=[pl.BlockSpec((1, tq, LANES), lambda b, p, i: (b, i, p)),
                  pl.BlockSpec((1, tq, LANES), lambda b, p, i: (b, i, 0)),
                  pl.BlockSpec((1, seq, LANES), lambda b, p, i: (b, 0, npair + p)),
                  pl.BlockSpec((1, seq, LANES), lambda b, p, i: (b, 0, 0)),
                  pl.BlockSpec((1, seq, LANES), lambda b, p, i: (b, 0, 2 * npair + p))],
        out_specs=pl.BlockSpec((1, tq, LANES), lambda b, p, i: (b, i, p)),
        out_shape=jax.ShapeDtypeStruct((bsz, seq, att_w), BF16),
        compiler_params=_params(("arbitrary", "arbitrary", "arbitrary")),
        name="fox_attention",
    )(qkv, qaug, qkv, kaug, qkv)


def _s5_weights(lam_re, lam_im, log_dt, b_re, b_im, c_re, c_im, d_skip):
    f32 = F32
    chunk = SUB
    lr, li = lam_re.astype(f32), lam_im.astype(f32)
    g, p = lr.shape
    hc = b_re.shape[-1]
    gl = LANES // hc
    nt = g // gl
    dt = jnp.exp(log_dt.astype(f32))[:, None]
    mag = jnp.exp(lr * dt)
    ang = li * dt
    a_re, a_im = mag * jnp.cos(ang), mag * jnp.sin(ang)
    n_re, n_im = a_re - 1.0, a_im
    den = lr * lr + li * li
    k_re = (n_re * lr + n_im * li) / den
    k_im = (n_im * lr - n_re * li) / den
    br, bi = b_re.astype(f32), b_im.astype(f32)
    bb_re = k_re[..., None] * br - k_im[..., None] * bi
    bb_im = k_re[..., None] * bi + k_im[..., None] * br
    cr, ci = c_re.astype(f32), c_im.astype(f32)

    tau = jnp.arange(chunk + 1, dtype=f32)
    pmag = jnp.exp((lr * dt)[..., None] * tau)
    pang = ang[..., None] * tau
    pr, pi = pmag * jnp.cos(pang), pmag * jnp.sin(pang)
    wr = cr[..., None] * pr[:, None] - ci[..., None] * pi[:, None]
    wi = cr[..., None] * pi[:, None] + ci[..., None] * pr[:, None]
    hp = lax.Precision.HIGHEST
    kern = (jnp.einsum('ghpt,gpk->gthk', wr, bb_re, precision=hp)
            - jnp.einsum('ghpt,gpk->gthk', wi, bb_im, precision=hp))
    ii = jnp.arange(chunk)
    diff = ii[None, :] - ii[:, None]
    kt = kern[:, jnp.clip(diff, 0, chunk)]
    kt = jnp.where((diff >= 0)[None, :, :, None, None], kt, 0.0)
    eye = jnp.eye(gl, dtype=f32)
    kt = kt.reshape(nt, gl, chunk, chunk, hc, hc)
    t_intra = jnp.einsum('tajihk,ab->tjakibh', kt, eye).reshape(nt, chunk * LANES, chunk * LANES)

    rev = chunk - 1 - ii
    pr_rev, pi_rev = pr[:, :, rev], pi[:, :, rev]
    s_re = pr_rev[..., None] * bb_re[:, :, None, :] - pi_rev[..., None] * bb_im[:, :, None, :]
    s_im = pr_rev[..., None] * bb_im[:, :, None, :] + pi_rev[..., None] * bb_re[:, :, None, :]
    st = jnp.stack([s_re, s_im], axis=1).reshape(nt, gl, 2, p, chunk, hc)
    b_state = jnp.einsum('tacpjk,ab->tjakcbp', st, eye).reshape(nt, chunk * LANES, 2 * gl * p)

    w1 = jnp.stack([wr[..., 1:], -wi[..., 1:]], axis=1).reshape(nt, gl, 2, hc, p, chunk)
    c_state = jnp.einsum('tachpi,ab->tcapibh', w1, eye).reshape(nt, 2 * gl * p, chunk * LANES)

    ac_re = pr[:, :, chunk].reshape(nt, 1, gl * p)
    ac_im = pi[:, :, chunk].reshape(nt, 1, gl * p)
    a1 = jnp.concatenate([ac_re, ac_re], axis=-1)
    a2 = jnp.concatenate([-ac_im, ac_im], axis=-1)
    dvec = d_skip.astype(f32).reshape(nt, 1, LANES)
    return t_intra.astype(BF16), b_state.astype(BF16), c_state.astype(BF16), a1, a2, dvec


def _s5_kernel(u_ref, t_ref, b_ref, c_ref, a1_ref, a2_ref, d_ref, y_ref, uf_ref, s_ref, *, bb, nk):
    nc = s_ref.shape[0]
    for b in range(bb):
        for j in range(SUB):
            uf_ref[b * nk:(b + 1) * nk, j * LANES:(j + 1) * LANES] = (
                u_ref[b, pl.ds(j, nk, stride=SUB), :].astype(BF16))
    for b in range(bb):
        rows = slice(b * nk, (b + 1) * nk)
        s_loc = _dot(uf_ref[rows, :], b_ref[0])
        for c in range(nc):
            s_ref[c, rows, :] = s_loc[:, c * LANES:(c + 1) * LANES]
    a1 = a1_ref[0]
    a2 = a2_ref[0]

    def carry_step(k, state):
        rows = pl.ds(k, bb, stride=nk)
        loc = jnp.concatenate([s_ref[c, rows, :] for c in range(nc)], axis=1)
        for c in range(nc):
            s_ref[c, rows, :] = state[:, c * LANES:(c + 1) * LANES]
        return a1 * state + a2 * pltpu.roll(state, nc * LANES // 2, 1) + loc

    lax.fori_loop(0, nk, carry_step, jnp.zeros((bb, nc * LANES), F32))
    d = d_ref[0]
    for b in range(bb):
        rows = slice(b * nk, (b + 1) * nk)
        s_prev = jnp.concatenate([s_ref[c, rows, :] for c in range(nc)], axis=1).astype(BF16)
        y = _dot(uf_ref[rows, :], t_ref[0]) + _dot(s_prev, c_ref[0])
        for i in range(SUB):
            yi = y[:, i * LANES:(i + 1) * LANES] + d * u_ref[b, pl.ds(i, nk, stride=SUB), :]
            y_ref[b, pl.ds(i, nk, stride=SUB), :] = jax.nn.gelu(yi)


def _s5(u, weights, *, bb):
    bsz, seq, w = u.shape
    t_intra, b_state, c_state, a1, a2, dvec = weights
    nt = t_intra.shape[0]
    nk = seq // SUB
    ch = SUB * LANES
    ns = b_state.shape[-1]
    kern = functools.partial(_s5_kernel, bb=bb, nk=nk)
    return pl.pallas_call(
        kern,
        grid=(nt, bsz // bb),
        in_specs=[pl.BlockSpec((bb, seq, LANES), lambda t, i: (i, 0, t)),
                  pl.BlockSpec((1, ch, ch), lambda t, i: (t, 0, 0)),
                  pl.BlockSpec((1, ch, ns), lambda t, i: (t, 0, 0)),
                  pl.BlockSpec((1, ns, ch), lambda t, i: (t, 0, 0)),
                  pl.BlockSpec((1, 1, ns), lambda t, i: (t, 0, 0)),
                  pl.BlockSpec((1, 1, ns), lambda t, i: (t, 0, 0)),
                  pl.BlockSpec((1, 1, LANES), lambda t, i: (t, 0, 0))],
        out_specs=pl.BlockSpec((bb, seq, LANES), lambda t, i: (i, 0, t)),
        out_shape=jax.ShapeDtypeStruct((bsz, seq, w), F32),
        scratch_shapes=[pltpu.VMEM((bb * nk, ch), BF16), pltpu.VMEM((ns // LANES, bb * nk, LANES), F32)],
        compiler_params=_params(("arbitrary", "arbitrary")),
        name="s5_chunked",
    )(u, t_intra, b_state, c_state, a1, a2, dvec)


def _route(logits):
    lane = lax.broadcasted_iota(jnp.int32, logits.shape, 1)
    valid = lane < N_EXPERTS
    neg = jnp.float32(-jnp.inf)
    lg = jnp.where(valid, logits, neg)
    m1 = jnp.max(lg, axis=-1, keepdims=True)
    i1 = jnp.min(jnp.where(lg == m1, lane, LANES), axis=-1, keepdims=True)
    lg2 = jnp.where(lane == i1, neg, lg)
    m2 = jnp.max(lg2, axis=-1, keepdims=True)
    i2 = jnp.min(jnp.where(lg2 == m2, lane, LANES), axis=-1, keepdims=True)
    e2 = jnp.exp(m2 - m1)
    den = 1.0 + e2
    gates = jnp.where(lane == i1, 1.0 / den, 0.0) + jnp.where(lane == i2, e2 / den, 0.0)
    return gates, (lane == i1) | (lane == i2)


def _mix_kernel(att_ref, ys_ref, gate_ref, x_ref, ada_ref, wpa_ref, wglu_ref, bglu_ref, wps_ref,
                wout_ref, gffn_ref, wrh_ref, wrl_ref, br_ref, xo_ref, h2_ref, *route_refs, d, tm, group):
    ada = ada_ref[0]
    ya = _dot(att_ref[0], wpa_ref[...])
    ys = ys_ref[0]
    glu = ys * jax.nn.sigmoid(_dot(ys.astype(BF16), wglu_ref[...]) + bglu_ref[...])
    yp = _dot(glu.astype(BF16), wps_ref[...])
    mixed = gate_ref[0, :, 0:d].astype(F32) * ya + gate_ref[0, :, d:2 * d].astype(F32) * yp
    x_new = x_ref[0] + ada[2:3] * _dot(mixed.astype(BF16), wout_ref[...])
    xo_ref[0] = x_new
    h2 = _modulated_norm(x_new, gffn_ref[...], ada[3:4], ada[4:5])
    h2b = h2.astype(BF16)
    h2_ref[0] = h2b
    if not route_refs:
        return
    rkc_ref, rkr_ref, gwr_ref, cnt_ref = route_refs
    h2l = (h2 - h2b.astype(F32)).astype(BF16)
    logits = (_dot(h2b, wrh_ref[...]) + _dot(h2b, wrl_ref[...]) + _dot(h2l, wrh_ref[...])) + br_ref[...]
    gates, mask = _route(logits)

    @pl.when(pl.program_id(1) % group == 0)
    def _():
        cnt_ref[...] = jnp.zeros_like(cnt_ref)

    row = lax.broadcasted_iota(jnp.int32, (tm, tm), 0)
    col = lax.broadcasted_iota(jnp.int32, (tm, tm), 1)
    maskf = mask.astype(F32)
    before = _dot((row > col).astype(BF16), maskf.astype(BF16)) + cnt_ref[0:1, :]
    rank = jnp.where(mask, before, -1.0)
    cnt_ref[...] = jnp.broadcast_to(before[tm - 1:tm, :] + maskf[tm - 1:tm, :], cnt_ref.shape)
    rkc_ref[0] = rank
    rkr_ref[...] = jnp.transpose(rank)[0:N_EXPERTS, :]
    gwr_ref[...] = jnp.transpose(gates)[0:N_EXPERTS, :]


def _mix(att, ys, gates, x, ada_l, wpa, wglu, bglu, wps, wout, gffn, wr_hi, wr_lo, b_router, *, tm, group):
    bsz, seq, d = x.shape
    aw = att.shape[-1]
    sw = ys.shape[-1]
    nj = seq // tm
    kern = functools.partial(_mix_kernel, d=d, tm=tm, group=group)
    const = lambda b, j: (0, 0)
    tile = lambda b, j: (b, j, 0)
    out_specs = [pl.BlockSpec((1, tm, d), tile), pl.BlockSpec((1, tm, d), tile)]
    out_shape = [jax.ShapeDtypeStruct((bsz, seq, d), F32), jax.ShapeDtypeStruct((bsz, seq, d), BF16)]
    scratch = []
    if group:
        flat = lambda b, j: (0, b * nj + j)
        out_specs += [pl.BlockSpec((1, tm, LANES), tile), pl.BlockSpec((N_EXPERTS, tm), flat),
                      pl.BlockSpec((N_EXPERTS, tm), flat)]
        out_shape += [jax.ShapeDtypeStruct((bsz, seq, LANES), F32),
                      jax.ShapeDtypeStruct((N_EXPERTS, bsz * seq), F32),
                      jax.ShapeDtypeStruct((N_EXPERTS, bsz * seq), F32)]
        scratch = [pltpu.VMEM((SUB, LANES), F32)]
    return pl.pallas_call(
        kern,
        grid=(bsz, nj),
        in_specs=[pl.BlockSpec((1, tm, aw), tile),
                  pl.BlockSpec((1, tm, sw), tile),
                  pl.BlockSpec((1, tm, 2 * d), tile),
                  pl.BlockSpec((1, tm, d), tile),
                  pl.BlockSpec((1, 6, d), lambda b, j: (b, 0, 0)),
                  pl.BlockSpec((aw, d), const),
                  pl.BlockSpec((sw, sw), const),
                  pl.BlockSpec((1, sw), const),
                  pl.BlockSpec((sw, d), const),
                  pl.BlockSpec((d, d), const),
                  pl.BlockSpec((1, d), const),
                  pl.BlockSpec((d, LANES), const),
                  pl.BlockSpec((d, LANES), const),
                  pl.BlockSpec((1, LANES), const)],
        out_specs=out_specs,
        out_shape=out_shape,
        scratch_shapes=scratch,
        compiler_params=_params(("arbitrary", "arbitrary")),
        name="mix",
    )(att, ys, gates, x, ada_l, wpa, wglu, bglu, wps, wout, gffn, wr_hi, wr_lo, b_router)


def _residual_out(x, ada, acc, gfin_ref, final_norm):
    y = x + ada[5:6] * acc
    if final_norm:
        y = y * lax.rsqrt(jnp.mean(y * y, axis=-1, keepdims=True) + EPS) * gfin_ref[...]
    return y


def _ffn_kernel(h_ref, x_ref, ada_ref, w1_ref, w3_ref, w2_ref, gfin_ref, o_ref, acc_ref, *, nf, final_norm):
    f = pl.program_id(1)

    @pl.when(f == 0)
    def _():
        acc_ref[...] = jnp.zeros_like(acc_ref)

    h = h_ref[...]
    act = jax.nn.silu(_dot(h, w1_ref[...])) * _dot(h, w3_ref[...])
    acc_ref[...] += _dot(act.astype(BF16), w2_ref[...])

    @pl.when(f == nf - 1)
    def _():
        o_ref[...] = _residual_out(x_ref[...], ada_ref[0], acc_ref[...], gfin_ref, final_norm)


def _ffn(h2, x, ada_l, w1, w3, w2, g_final, *, tm, cf, seq, final_norm):
    t, d = h2.shape
    dff = w1.shape[-1]
    nf = dff // cf
    tiles_per_seq = seq // tm
    kern = functools.partial(_ffn_kernel, nf=nf, final_norm=final_norm)
    return pl.pallas_call(
        kern,
        grid=(t // tm, nf),
        in_specs=[pl.BlockSpec((tm, d), lambda i, f: (i, 0)),
                  pl.BlockSpec((tm, d), lambda i, f: (i, 0)),
                  pl.BlockSpec((1, 6, d), lambda i, f: (i // tiles_per_seq, 0, 0)),
                  pl.BlockSpec((d, cf), lambda i, f: (0, f)),
                  pl.BlockSpec((d, cf), lambda i, f: (0, f)),
                  pl.BlockSpec((cf, d), lambda i, f: (f, 0)),
                  pl.BlockSpec((1, d), lambda i, f: (0, 0))],
        out_specs=pl.BlockSpec((tm, d), lambda i, f: (i, 0)),
        out_shape=jax.ShapeDtypeStruct((t, d), F32),
        scratch_shapes=[pltpu.VMEM((tm, d), F32)],
        compiler_params=_params(("arbitrary", "arbitrary")),
        name="ffn",
    )(h2, x, ada_l, w1, w3, w2, g_final)


def _moe_kernel(h_ref, rkc_ref, rkr_ref, gwr_ref, x_ref, ada_ref, w1_ref, w3_ref, w2_ref, gfin_ref, o_ref,
                *, n_exp, rows, final_norm):
    e = pl.program_id(1)
    tf = h_ref.shape[0]

    @pl.when(e == 0)
    def _():
        o_ref[...] = jnp.zeros_like(o_ref)

    lane = lax.broadcasted_iota(jnp.int32, (tf, LANES), 1)
    rk_col = jnp.sum(jnp.where(lane == e, rkc_ref[...], 0.0), axis=-1, keepdims=True)
    rk_row = rkr_ref[pl.ds(e, 1), :]
    gw_row = gwr_ref[pl.ds(e, 1), :]
    count = jnp.max(rk_row).astype(jnp.int32) + 1
    r_sub = lax.broadcasted_iota(jnp.int32, (rows, tf), 0).astype(F32)
    r_lane = lax.broadcasted_iota(jnp.int32, (tf, rows), 1).astype(F32)

    def chunk(c, carry):
        base = (c * rows).astype(F32)
        pick = rk_row == r_sub + base
        xc = _dot(pick.astype(BF16), h_ref[...]).astype(BF16)
        gate = jnp.sum(jnp.where(pick, gw_row, 0.0), axis=-1, keepdims=True)
        act = jax.nn.silu(_dot(xc, w1_ref[0])) * _dot(xc, w3_ref[0]) * gate
        yc = _dot(act.astype(BF16), w2_ref[0]).astype(BF16)
        o_ref[...] += _dot((rk_col == r_lane + base).astype(BF16), yc)
        return carry

    lax.fori_loop(0, (count + rows - 1) // rows, chunk, 0)

    @pl.when(e == n_exp - 1)
    def _():
        o_ref[...] = _residual_out(x_ref[...], ada_ref[0], o_ref[...], gfin_ref, final_norm)


def _moe(h2, rk_col, rk_row, gw_row, x, ada_l, w1, w3, w2, g_final, *, tf, rows, seq, final_norm):
    t, d = h2.shape
    n_exp, _, dff = w1.shape
    tiles_per_seq = seq // tf
    kern = functools.partial(_moe_kernel, n_exp=n_exp, rows=rows, final_norm=final_norm)
    once = pl.Buffered(1)
    return pl.pallas_call(
        kern,
        grid=(t // tf, n_exp),
        in_specs=[pl.BlockSpec((tf, d), lambda i, e: (i, 0), pipeline_mode=once),
                  pl.BlockSpec((tf, LANES), lambda i, e: (i, 0), pipeline_mode=once),
                  pl.BlockSpec((N_EXPERTS, tf), lambda i, e: (0, i), pipeline_mode=once),
                  pl.BlockSpec((N_EXPERTS, tf), lambda i, e: (0, i), pipeline_mode=once),
                  pl.BlockSpec((tf, d), lambda i, e: (i, 0), pipeline_mode=once),
                  pl.BlockSpec((1, 6, d), lambda i, e: (i // tiles_per_seq, 0, 0)),
                  pl.BlockSpec((1, d, dff), lambda i, e: (e, 0, 0)),
                  pl.BlockSpec((1, d, dff), lambda i, e: (e, 0, 0)),
                  pl.BlockSpec((1, dff, d), lambda i, e: (e, 0, 0)),
                  pl.BlockSpec((1, d), lambda i, e: (0, 0))],
        out_specs=pl.BlockSpec((tf, d), lambda i, e: (i, 0)),
        out_shape=jax.ShapeDtypeStruct((t, d), F32),
        compiler_params=_params(("arbitrary", "arbitrary")),
        name="moe",
    )(h2, rk_col, rk_row, gw_row, x, ada_l, w1, w3, w2, g_final)


def _tiles(seq, bsz):
    tm = min(512, seq)
    tq = min(1024, seq // 2)
    tf = min(1024, seq)
    return tm, tq, min(4, bsz), tf, 128


def kernel(x, c, w_ada, b_ada, g_mix, w_in, b_forget, b_gate, lam_re, lam_im, log_dt, b_re, b_im,
           c_re, c_im, d_skip, w_glu, b_glu, w_proj_att, w_proj_ssm, w_out, g_ffn, w1_dense, w3_dense,
           w2_dense, w_router, b_router, w1_moe, w3_moe, w2_moe, g_final):
    bsz, seq, d = x.shape
    depth = w_ada.shape[0]
    att_w = ATT_HEADS * ATT_HEAD_DIM
    ssm_w = w_glu.shape[-1]
    dff = w1_dense.shape[-1]
    tm, tq, bb, tf, rows = _tiles(seq, bsz)
    cf = dff // 2

    ada = _ada(c, w_ada, b_ada).reshape(depth, bsz, 6, d)
    gfin = g_final.reshape(1, d)

    for l in range(depth):
        o1, o2, o3 = 3 * att_w, 3 * att_w + ATT_HEADS, 3 * att_w + ATT_HEADS + ssm_w
        w = w_in[l]
        w_fg = jnp.pad(w[:, o1:o2], ((0, 0), (0, LANES - ATT_HEADS)))
        w_all = jnp.concatenate([w[:, :o1], w[:, o2:o3], w[:, o3:], w_fg], axis=1).astype(BF16)
        bf_pad = jnp.pad(b_forget[l], (0, LANES - ATT_HEADS)).reshape(1, LANES)

        qkv, u, gates, qaug, kaug = _inproj(x, ada[l], g_mix[l].reshape(1, d), w_all,
                                      b_gate[l].reshape(1, 2 * d), bf_pad,
                                      tm=tm, att_w=att_w, ssm_w=ssm_w)

        att = _attention(qkv, qaug, kaug, tq=tq)

        s5w = _s5_weights(lam_re[l], lam_im[l], log_dt[l], b_re[l], b_im[l], c_re[l], c_im[l], d_skip[l])
        ys = _s5(u, s5w, bb=bb)

        moe = l % 2 == 1
        m = l // 2
        if moe:
            wr = jnp.pad(w_router[m], ((0, 0), (0, LANES - N_EXPERTS)))
            br = jnp.pad(b_router[m], (0, LANES - N_EXPERTS)).reshape(1, LANES)
        else:
            wr = jnp.zeros((d, LANES), F32)
            br = jnp.zeros((1, LANES), F32)
        wr_hi = wr.astype(BF16)
        wr_lo = (wr - wr_hi.astype(F32)).astype(BF16)
        mixed = _mix(att, ys, gates, x, ada[l],
                     w_proj_att[l].astype(BF16), w_glu[l].astype(BF16),
                     b_glu[l].reshape(1, ssm_w), w_proj_ssm[l].astype(BF16),
                     w_out[l].astype(BF16), g_ffn[l].reshape(1, d), wr_hi, wr_lo, br,
                     tm=tm, group=(tf // tm if moe else 0))
        x, h2 = mixed[0].reshape(bsz * seq, d), mixed[1].reshape(bsz * seq, d)
        last = l == depth - 1
        if moe:
            x = _moe(h2, mixed[2].reshape(bsz * seq, LANES), mixed[3], mixed[4], x, ada[l],
                     w1_moe[m].astype(BF16), w3_moe[m].astype(BF16), w2_moe[m].astype(BF16), gfin,
                     tf=tf, rows=rows, seq=seq, final_norm=last)
        else:
            x = _ffn(h2, x, ada[l], w1_dense[m].astype(BF16), w3_dense[m].astype(BF16),
                     w2_dense[m].astype(BF16), gfin, tm=tm, cf=cf, seq=seq, final_norm=last)
        x = x.reshape(bsz, seq, d)
    return x
```

```python
import functools
import math

import jax
import jax.numpy as jnp
from jax import lax
from jax.experimental import pallas as pl
from jax.experimental.pallas import tpu as pltpu

F32 = jnp.float32
BF16 = jnp.bfloat16
EPS = 1e-6

ATT_HEADS = 8
ATT_HEAD_DIM = 64
SSM_GROUP = 16
SSM_STATE = 64
N_EXPERTS = 8
LANES = 128
SUB = 8
LOG2E = math.log2(math.e)
F_OFF = 32
VMEM_LIMIT = 56 * 1024 * 1024


def _params(sem):
    return pltpu.CompilerParams(dimension_semantics=sem, vmem_limit_bytes=VMEM_LIMIT)


def _dot(a, b):
    return jnp.dot(a, b, preferred_element_type=F32)


def _ada_kernel(c_ref, w_ref, b_ref, o_ref):
    cond = jax.nn.silu(c_ref[...])
    o_ref[0] = jnp.dot(cond, w_ref[0], preferred_element_type=F32,
                       precision=lax.Precision.HIGHEST) + b_ref[0]


def _ada(c, w_ada, b_ada, tn=1536):
    depth, d, n = w_ada.shape
    bsz = c.shape[0]
    return pl.pallas_call(
        _ada_kernel,
        grid=(depth, n // tn),
        in_specs=[pl.BlockSpec((bsz, d), lambda l, j: (0, 0)),
                  pl.BlockSpec((1, d, tn), lambda l, j: (l, 0, j)),
                  pl.BlockSpec((1, 1, tn), lambda l, j: (l, 0, j))],
        out_specs=pl.BlockSpec((1, bsz, tn), lambda l, j: (l, 0, j)),
        out_shape=jax.ShapeDtypeStruct((depth, bsz, n), F32),
        compiler_params=_params(("arbitrary", "arbitrary")),
        name="ada",
    )(c, w_ada, b_ada.reshape(depth, 1, n))


def _modulated_norm(x, g, shift, scale):
    y = x * lax.rsqrt(jnp.mean(x * x, axis=-1, keepdims=True) + EPS) * g
    return y * (1.0 + scale) + shift


def _inproj_kernel(x_ref, ada_ref, g_ref, w_ref, bg_ref, bf_ref,
                   qkv_ref, u_ref, gate_ref, qa_ref, ka_ref, carry_ref, *, tm, att_w, ssm_w, d):
    j = pl.program_id(1)
    ada = ada_ref[0]
    h = _modulated_norm(x_ref[0], g_ref[...], ada[0:1], ada[1:2]).astype(BF16)

    cw = att_w
    qkv_ref[0, :, 0:cw] = (_dot(h, w_ref[:, 0:cw]) * (ATT_HEAD_DIM ** -0.5 * LOG2E)).astype(BF16)
    for c0 in range(cw, 3 * att_w, cw):
        qkv_ref[0, :, c0:c0 + cw] = _dot(h, w_ref[:, c0:c0 + cw]).astype(BF16)
    o_u = 3 * att_w
    u_ref[0] = _dot(h, w_ref[:, o_u:o_u + ssm_w])
    o_g = o_u + ssm_w
    for c0 in range(0, 2 * d, cw):
        z = _dot(h, w_ref[:, o_g + c0:o_g + c0 + cw]) + bg_ref[:, c0:c0 + cw]
        gate_ref[0, :, c0:c0 + cw] = jax.nn.sigmoid(z).astype(BF16)

    o_f = o_g + 2 * d
    logf = jax.nn.log_sigmoid(_dot(h, w_ref[:, o_f:o_f + LANES]) + bf_ref[...])
    row = lax.broadcasted_iota(jnp.int32, (tm, tm), 0)
    col = lax.broadcasted_iota(jnp.int32, (tm, tm), 1)
    tri = (row >= col).astype(BF16)
    hi = logf.astype(BF16)
    r1 = logf - hi.astype(F32)
    mid = r1.astype(BF16)
    lo = (r1 - mid.astype(F32)).astype(BF16)
    cs = _dot(tri, hi) + _dot(tri, mid) + _dot(tri, lo)

    @pl.when(j == 0)
    def _():
        carry_ref[...] = jnp.zeros_like(carry_ref)

    fcum = cs + carry_ref[0:1, :]
    carry_ref[...] = jnp.broadcast_to(fcum[tm - 1:tm, :], carry_ref.shape)

    lane = lax.broadcasted_iota(jnp.int32, (tm, LANES), 1)
    fl = jnp.where(lane < ATT_HEADS, fcum * LOG2E, 0.0)
    p0 = fl.astype(BF16).astype(F32)
    r0 = fl - p0
    p1 = r0.astype(BF16).astype(F32)
    p2 = (r0 - p1).astype(BF16).astype(F32)
    comb = p0 + pltpu.roll(p1, ATT_HEADS, 1) + pltpu.roll(p2, 2 * ATT_HEADS, 1)
    sel_lo = (lane < 3 * ATT_HEADS).astype(F32)
    sel_hi = ((lane >= F_OFF) & (lane < F_OFF + 3 * ATT_HEADS)).astype(F32)
    qa_ref[0] = (pltpu.roll(comb, F_OFF, 1) + sel_lo).astype(BF16)
    ka_ref[0] = (sel_hi - comb).astype(BF16)


def _inproj(x, ada_l, g, w_all, b_gate, b_forget_pad, *, tm, att_w, ssm_w):
    bsz, seq, d = x.shape
    n_all = w_all.shape[1]
    kern = functools.partial(_inproj_kernel, tm=tm, att_w=att_w, ssm_w=ssm_w, d=d)
    return pl.pallas_call(
        kern,
        grid=(bsz, seq // tm),
        in_specs=[pl.BlockSpec((1, tm, d), lambda b, j: (b, j, 0)),
                  pl.BlockSpec((1, 6, d), lambda b, j: (b, 0, 0)),
                  pl.BlockSpec((1, d), lambda b, j: (0, 0)),
                  pl.BlockSpec((d, n_all), lambda b, j: (0, 0)),
                  pl.BlockSpec((1, 2 * d), lambda b, j: (0, 0)),
                  pl.BlockSpec((1, LANES), lambda b, j: (0, 0))],
        out_specs=[pl.BlockSpec((1, tm, 3 * att_w), lambda b, j: (b, j, 0)),
                   pl.BlockSpec((1, tm, ssm_w), lambda b, j: (b, j, 0)),
                   pl.BlockSpec((1, tm, 2 * d), lambda b, j: (b, j, 0)),
                   pl.BlockSpec((1, tm, LANES), lambda b, j: (b, j, 0)),
                   pl.BlockSpec((1, tm, LANES), lambda b, j: (b, j, 0))],
        out_shape=[jax.ShapeDtypeStruct((bsz, seq, 3 * att_w), BF16),
                   jax.ShapeDtypeStruct((bsz, seq, ssm_w), F32),
                   jax.ShapeDtypeStruct((bsz, seq, 2 * d), BF16),
                   jax.ShapeDtypeStruct((bsz, seq, LANES), BF16),
                   jax.ShapeDtypeStruct((bsz, seq, LANES), BF16)],
        scratch_shapes=[pltpu.VMEM((8, LANES), F32)],
        compiler_params=_params(("arbitrary", "arbitrary")),
        name="inproj",
    )(x, ada_l, g, w_all, b_gate, b_forget_pad)


def _attn_kernel(q_ref, qa_ref, k_ref, ka_ref, v_ref, o_ref, *, tq):
    hp = pl.program_id(1)
    i = pl.program_id(2)
    half = tq // 2
    lane = lax.broadcasted_iota(jnp.int32, (tq, LANES), 1)
    first = lane < ATT_HEAD_DIM
    q2 = q_ref[0]
    qa = qa_ref[0]
    zero = jnp.zeros_like(q2)
    heads = []
    for hh in range(2):
        qsel = jnp.where(first, q2, zero) if hh == 0 else jnp.where(first, zero, q2)
        asel = jnp.where((lane & (ATT_HEADS - 1)) == 2 * hp + hh, qa, zero)
        heads.append(jnp.concatenate([qsel, asel], axis=1))

    def kv(start, size):
        kj = jnp.concatenate([k_ref[0, pl.ds(start, size), :], ka_ref[0, pl.ds(start, size), :]], axis=1)
        v2 = v_ref[0, pl.ds(start, size), :]
        one = jnp.ones_like(v2)
        lk = lax.broadcasted_iota(jnp.int32, v2.shape, 1) < ATT_HEAD_DIM
        return kj, (jnp.where(lk, v2, one), jnp.where(lk, one, v2))

    def update(qh, kj, vh, m, acc, mask):
        s = lax.dot_general(qh, kj, (((1,), (1,)), ((), ())), preferred_element_type=F32)
        if mask is not None:
            s = jnp.where(mask, s, -jnp.inf)
        m_new = jnp.maximum(m, jnp.max(s, axis=-1, keepdims=True))
        p = jnp.exp2(s - m_new).astype(BF16)
        acc = jnp.exp2(m - m_new) * acc + _dot(p, vh)
        return m_new, acc

    def body(j, carry):
        kj, vs = kv(pl.multiple_of(j * tq, tq), tq)
        out = []
        for hh in range(2):
            out.extend(update(heads[hh], kj, vs[hh], carry[2 * hh], carry[2 * hh + 1], None))
        return tuple(out)

    init = []
    for hh in range(2):
        init += [jnp.full((tq, 1), -jnp.inf, F32), jnp.zeros((tq, LANES), F32)]
    carry = lax.fori_loop(0, i, body, tuple(init))

    base = pl.multiple_of(i * tq, tq)
    r_top = lax.broadcasted_iota(jnp.int32, (half, half), 0)
    c_top = lax.broadcasted_iota(jnp.int32, (half, half), 1)
    r_bot = lax.broadcasted_iota(jnp.int32, (half, tq), 0)
    c_bot = lax.broadcasted_iota(jnp.int32, (half, tq), 1)
    k_top, v_top = kv(base, half)
    k_all, v_all = kv(base, tq)
    outs = []
    for hh in range(2):
        m, acc = carry[2 * hh], carry[2 * hh + 1]
        qh = heads[hh]
        _, acc_t = update(qh[:half], k_top, v_top[hh], m[:half], acc[:half], r_top >= c_top)
        _, acc_b = update(qh[half:], k_all, v_all[hh], m[half:], acc[half:], r_bot + half >= c_bot)
        acc = jnp.concatenate([acc_t, acc_b], axis=0)
        outs.append(acc / pltpu.roll(acc, ATT_HEAD_DIM, 1))
    o_ref[0] = jnp.where(first, outs[0], outs[1]).astype(BF16)


def _attention(qkv, qaug, kaug, *, tq):
    bsz, seq, w3 = qkv.shape
    att_w = w3 // 3
    npair = att_w // LANES
    kern = functools.partial(_attn_kernel, tq=tq)
    return pl.pallas_call(
        kern,
        grid=(bsz, npair, seq // tq),
        in_specs=[pl.BlockSpec((1, tq, LANES), lambda b, p, i: (b, i, p)),
                  pl.BlockSpec((1, tq, LANES), lambda b, p, i: (b, i, 0)),
                  pl.BlockSpec((1, seq, LANES), lambda b, p, i: (b, 0, npair + p)),
                  pl.BlockSpec((1, seq, LANES), lambda b, p, i: (b, 0, 0)),
                  pl.BlockSpec((1, seq, LANES), lambda b, p, i: (b, 0, 2 * npair + p))],
        out_specs=pl.BlockSpec((1, tq, LANES), lambda b, p, i: (b, i, p)),
        out_shape=jax.ShapeDtypeStruct((bsz, seq, att_w), BF16),
        compiler_params=_params(("arbitrary", "arbitrary", "arbitrary")),
        name="fox_attention",
    )(qkv, qaug, qkv, kaug, qkv)


def _split(x):
    hi = x.astype(BF16)
    return hi, (x - hi.astype(F32)).astype(BF16)


def _dot_nt3(a, b):
    dims = (((1,), (1,)), ((), ()))
    ah, al = _split(a)
    bh, bl = _split(b)
    nt = lambda u, v: lax.dot_general(u, v, dims, preferred_element_type=F32)
    return nt(ah, bh) + nt(ah, bl) + nt(al, bh)


def _dot_rep3(x, rep):
    p0 = x.astype(BF16)
    r0 = x - p0.astype(F32)
    p1 = r0.astype(BF16)
    p2 = (r0 - p1.astype(F32)).astype(BF16)
    return _dot(p0, rep) + _dot(p1, rep) + _dot(p2, rep)


def _s5_operator_kernel(lr_ref, li_ref, dt_ref, br_ref, bi_ref, cr_ref, ci_ref,
                        t_ref, b_ref, ct_ref, a1_ref, a2_ref, *, hc, p):
    gl = LANES // hc
    half = gl * p
    lr, li, dt = lr_ref[0], li_ref[0], jnp.exp(dt_ref[0])
    br, bi, cr, ci = br_ref[0], bi_ref[0], cr_ref[0], ci_ref[0]
    mag = jnp.exp(lr * dt)
    ang = li * dt
    a_re, a_im = mag * jnp.cos(ang), mag * jnp.sin(ang)
    n_re, n_im = a_re - 1.0, a_im
    den = lr * lr + li * li
    k_re = (n_re * lr + n_im * li) / den
    k_im = (n_im * lr - n_re * li) / den
    bb_re = k_re * br - k_im * bi
    bb_im = k_re * bi + k_im * br

    def power(t):
        m = jnp.exp(lr * dt * t)
        return m * jnp.cos(ang * t), m * jnp.sin(ang * t)

    pw = [power(float(t)) for t in range(SUB + 1)]

    rep = (lax.broadcasted_iota(jnp.int32, (p, half), 1) % p
           == lax.broadcasted_iota(jnp.int32, (p, half), 0)).astype(BF16)
    own = (lax.broadcasted_iota(jnp.int32, (LANES, half), 0) // hc
           == lax.broadcasted_iota(jnp.int32, (LANES, half), 1) // p)

    def expand(x):
        return jnp.where(own, _dot(x.astype(BF16), rep), 0.0).astype(BF16)

    same = (lax.broadcasted_iota(jnp.int32, (LANES, LANES), 0) // hc
            == lax.broadcasted_iota(jnp.int32, (LANES, LANES), 1) // hc)
    zero = jnp.zeros((LANES, LANES), BF16)
    kern = []
    for t in range(SUB):
        pr, pi = pw[t]
        wr, wi = cr * pr - ci * pi, cr * pi + ci * pr
        k = _dot_nt3(bb_re, wr) - _dot_nt3(bb_im, wi)
        kern.append(jnp.where(same, k, 0.0).astype(BF16))
    for j in range(SUB):
        for i in range(SUB):
            t_ref[0, j * LANES:(j + 1) * LANES, i * LANES:(i + 1) * LANES] = kern[i - j] if i >= j else zero
        pr, pi = pw[SUB - 1 - j]
        b_ref[0, j * LANES:(j + 1) * LANES, 0:half] = expand(pr * bb_re - pi * bb_im)
        b_ref[0, j * LANES:(j + 1) * LANES, half:2 * half] = expand(pr * bb_im + pi * bb_re)
        pr, pi = pw[j + 1]
        ct_ref[0, j * LANES:(j + 1) * LANES, 0:half] = expand(cr * pr - ci * pi)
        ct_ref[0, j * LANES:(j + 1) * LANES, half:2 * half] = expand(-(cr * pi + ci * pr))
    pr, pi = pw[SUB]
    along = lambda x: jnp.sum(jnp.where(own, _dot_rep3(x, rep), 0.0), axis=0, keepdims=True) * (1.0 / hc)
    ac_re, ac_im = along(pr), along(pi)
    a1_ref[0] = jnp.concatenate([ac_re, ac_re], axis=1)
    a2_ref[0] = jnp.concatenate([-ac_im, ac_im], axis=1)


def _s5_operators(lam_re, lam_im, log_dt, b_re, b_im, c_re, c_im):
    g, p = lam_re.shape
    hc = b_re.shape[-1]
    nt = g * hc // LANES
    ch = SUB * LANES
    ns = 2 * (LANES // hc) * p
    rows = lambda v: jnp.broadcast_to(v[:, None, :], (g, hc, v.shape[-1])).reshape(nt, LANES, v.shape[-1])
    compact = [rows(lam_re.astype(F32)), rows(lam_im.astype(F32)),
               rows(jnp.broadcast_to(log_dt.astype(F32)[:, None], (g, p))),
               b_re.astype(F32).transpose(0, 2, 1).reshape(nt, LANES, p),
               b_im.astype(F32).transpose(0, 2, 1).reshape(nt, LANES, p),
               c_re.astype(F32).reshape(nt, LANES, p), c_im.astype(F32).reshape(nt, LANES, p)]
    kern = functools.partial(_s5_operator_kernel, hc=hc, p=p)
    return pl.pallas_call(
        kern,
        grid=(nt,),
        in_specs=[pl.BlockSpec((1, LANES, p), lambda t: (t, 0, 0))] * 7,
        out_specs=[pl.BlockSpec((1, ch, ch), lambda t: (t, 0, 0)),
                   pl.BlockSpec((1, ch, ns), lambda t: (t, 0, 0)),
                   pl.BlockSpec((1, ch, ns), lambda t: (t, 0, 0)),
                   pl.BlockSpec((1, 1, ns), lambda t: (t, 0, 0)),
                   pl.BlockSpec((1, 1, ns), lambda t: (t, 0, 0))],
        out_shape=[jax.ShapeDtypeStruct((nt, ch, ch), BF16), jax.ShapeDtypeStruct((nt, ch, ns), BF16),
                   jax.ShapeDtypeStruct((nt, ch, ns), BF16), jax.ShapeDtypeStruct((nt, 1, ns), F32),
                   jax.ShapeDtypeStruct((nt, 1, ns), F32)],
        compiler_params=_params(("arbitrary",)),
        name="s5_operators",
    )(*compact)


def _s5_kernel(u_ref, t_ref, b_ref, ct_ref, a1_ref, a2_ref, d_ref, y_ref, uf_ref, s_ref, *, bb, nk):
    nc = s_ref.shape[0]
    for b in range(bb):
        for j in range(SUB):
            uf_ref[b * nk:(b + 1) * nk, j * LANES:(j + 1) * LANES] = (
                u_ref[b, pl.ds(j, nk, stride=SUB), :].astype(BF16))
    for b in range(bb):
        rows = slice(b * nk, (b + 1) * nk)
        s_loc = _dot(uf_ref[rows, :], b_ref[0])
        for c in range(nc):
            s_ref[c, rows, :] = s_loc[:, c * LANES:(c + 1) * LANES]
    a1 = a1_ref[0]
    a2 = a2_ref[0]

    def carry_step(k, state):
        rows = pl.ds(k, bb, stride=nk)
        loc = jnp.concatenate([s_ref[c, rows, :] for c in range(nc)], axis=1)
        for c in range(nc):
            s_ref[c, rows, :] = state[:, c * LANES:(c + 1) * LANES]
        return a1 * state + a2 * pltpu.roll(state, nc * LANES // 2, 1) + loc

    lax.fori_loop(0, nk, carry_step, jnp.zeros((bb, nc * LANES), F32))
    d = d_ref[0]
    for b in range(bb):
        rows = slice(b * nk, (b + 1) * nk)
        s_prev = jnp.concatenate([s_ref[c, rows, :] for c in range(nc)], axis=1).astype(BF16)
        y = _dot(uf_ref[rows, :], t_ref[0]) + lax.dot_general(
            s_prev, ct_ref[0], (((1,), (1,)), ((), ())), preferred_element_type=F32)
        for i in range(SUB):
            yi = y[:, i * LANES:(i + 1) * LANES] + d * u_ref[b, pl.ds(i, nk, stride=SUB), :]
            y_ref[b, pl.ds(i, nk, stride=SUB), :] = jax.nn.gelu(yi)


def _s5(u, weights, *, bb):
    bsz, seq, w = u.shape
    t_intra, b_state, c_state, a1, a2, dvec = weights
    nt = t_intra.shape[0]
    nk = seq // SUB
    ch = SUB * LANES
    ns = b_state.shape[-1]
    kern = functools.partial(_s5_kernel, bb=bb, nk=nk)
    return pl.pallas_call(
        kern,
        grid=(nt, bsz // bb),
        in_specs=[pl.BlockSpec((bb, seq, LANES), lambda t, i: (i, 0, t)),
                  pl.BlockSpec((1, ch, ch), lambda t, i: (t, 0, 0)),
                  pl.BlockSpec((1, ch, ns), lambda t, i: (t, 0, 0)),
                  pl.BlockSpec((1, ns, ch), lambda t, i: (t, 0, 0)),
                  pl.BlockSpec((1, 1, ns), lambda t, i: (t, 0, 0)),
                  pl.BlockSpec((1, 1, ns), lambda t, i: (t, 0, 0)),
                  pl.BlockSpec((1, 1, LANES), lambda t, i: (t, 0, 0))],
        out_specs=pl.BlockSpec((bb, seq, LANES), lambda t, i: (i, 0, t)),
        out_shape=jax.ShapeDtypeStruct((bsz, seq, w), F32),
        scratch_shapes=[pltpu.VMEM((bb * nk, ch), BF16), pltpu.VMEM((ns // LANES, bb * nk, LANES), F32)],
        compiler_params=_params(("arbitrary", "arbitrary")),
        name="s5_chunked",
    )(u, t_intra, b_state, c_state, a1, a2, dvec)


def _route(logits):
    lane = lax.broadcasted_iota(jnp.int32, logits.shape, 1)
    valid = lane < N_EXPERTS
    neg = jnp.float32(-jnp.inf)
    lg = jnp.where(valid, logits, neg)
    m1 = jnp.max(lg, axis=-1, keepdims=True)
    i1 = jnp.min(jnp.where(lg == m1, lane, LANES), axis=-1, keepdims=True)
    lg2 = jnp.where(lane == i1, neg, lg)
    m2 = jnp.max(lg2, axis=-1, keepdims=True)
    i2 = jnp.min(jnp.where(lg2 == m2, lane, LANES), axis=-1, keepdims=True)
    e2 = jnp.exp(m2 - m1)
    den = 1.0 + e2
    gates = jnp.where(lane == i1, 1.0 / den, 0.0) + jnp.where(lane == i2, e2 / den, 0.0)
    return gates, (lane == i1) | (lane == i2)


def _mix_kernel(att_ref, ys_ref, gate_ref, x_ref, ada_ref, wpa_ref, wglu_ref, bglu_ref, wps_ref,
                wout_ref, gffn_ref, wrh_ref, wrl_ref, br_ref, xo_ref, h2_ref, *route_refs, d, tm, group):
    ada = ada_ref[0]
    ya = _dot(att_ref[0], wpa_ref[...])
    ys = ys_ref[0]
    glu = ys * jax.nn.sigmoid(_dot(ys.astype(BF16), wglu_ref[...]) + bglu_ref[...])
    yp = _dot(glu.astype(BF16), wps_ref[...])
    mixed = gate_ref[0, :, 0:d].astype(F32) * ya + gate_ref[0, :, d:2 * d].astype(F32) * yp
    x_new = x_ref[0] + ada[2:3] * _dot(mixed.astype(BF16), wout_ref[...])
    xo_ref[0] = x_new
    h2 = _modulated_norm(x_new, gffn_ref[...], ada[3:4], ada[4:5])
    h2b = h2.astype(BF16)
    h2_ref[0] = h2b
    if not route_refs:
        return
    rkc_ref, rkr_ref, gwr_ref, cnt_ref = route_refs
    h2l = (h2 - h2b.astype(F32)).astype(BF16)
    logits = (_dot(h2b, wrh_ref[...]) + _dot(h2b, wrl_ref[...]) + _dot(h2l, wrh_ref[...])) + br_ref[...]
    gates, mask = _route(logits)

    @pl.when(pl.program_id(1) % group == 0)
    def _():
        cnt_ref[...] = jnp.zeros_like(cnt_ref)

    row = lax.broadcasted_iota(jnp.int32, (tm, tm), 0)
    col = lax.broadcasted_iota(jnp.int32, (tm, tm), 1)
    maskf = mask.astype(F32)
    before = _dot((row > col).astype(BF16), maskf.astype(BF16)) + cnt_ref[0:1, :]
    rank = jnp.where(mask, before, -1.0)
    cnt_ref[...] = jnp.broadcast_to(before[tm - 1:tm, :] + maskf[tm - 1:tm, :], cnt_ref.shape)
    rkc_ref[0] = rank
    rkr_ref[...] = jnp.transpose(rank)[0:N_EXPERTS, :]
    gwr_ref[...] = jnp.transpose(gates)[0:N_EXPERTS, :]


def _mix(att, ys, gates, x, ada_l, wpa, wglu, bglu, wps, wout, gffn, wr_hi, wr_lo, b_router, *, tm, group):
    bsz, seq, d = x.shape
    aw = att.shape[-1]
    sw = ys.shape[-1]
    nj = seq // tm
    kern = functools.partial(_mix_kernel, d=d, tm=tm, group=group)
    const = lambda b, j: (0, 0)
    tile = lambda b, j: (b, j, 0)
    out_specs = [pl.BlockSpec((1, tm, d), tile), pl.BlockSpec((1, tm, d), tile)]
    out_shape = [jax.ShapeDtypeStruct((bsz, seq, d), F32), jax.ShapeDtypeStruct((bsz, seq, d), BF16)]
    scratch = []
    if group:
        flat = lambda b, j: (0, b * nj + j)
        out_specs += [pl.BlockSpec((1, tm, LANES), tile), pl.BlockSpec((N_EXPERTS, tm), flat),
                      pl.BlockSpec((N_EXPERTS, tm), flat)]
        out_shape += [jax.ShapeDtypeStruct((bsz, seq, LANES), F32),
                      jax.ShapeDtypeStruct((N_EXPERTS, bsz * seq), F32),
                      jax.ShapeDtypeStruct((N_EXPERTS, bsz * seq), F32)]
        scratch = [pltpu.VMEM((SUB, LANES), F32)]
    return pl.pallas_call(
        kern,
        grid=(bsz, nj),
        in_specs=[pl.BlockSpec((1, tm, aw), tile),
                  pl.BlockSpec((1, tm, sw), tile),
                  pl.BlockSpec((1, tm, 2 * d), tile),
                  pl.BlockSpec((1, tm, d), tile),
                  pl.BlockSpec((1, 6, d), lambda b, j: (b, 0, 0)),
                  pl.BlockSpec((aw, d), const),
                  pl.BlockSpec((sw, sw), const),
                  pl.BlockSpec((1, sw), const),
                  pl.BlockSpec((sw, d), const),
                  pl.BlockSpec((d, d), const),
                  pl.BlockSpec((1, d), const),
                  pl.BlockSpec((d, LANES), const),
                  pl.BlockSpec((d, LANES), const),
                  pl.BlockSpec((1, LANES), const)],
        out_specs=out_specs,
        out_shape=out_shape,
        scratch_shapes=scratch,
        compiler_params=_params(("arbitrary", "arbitrary")),
        name="mix",
    )(att, ys, gates, x, ada_l, wpa, wglu, bglu, wps, wout, gffn, wr_hi, wr_lo, b_router)


def _residual_out(x, ada, acc, gfin_ref, final_norm):
    y = x + ada[5:6] * acc
    if final_norm:
        y = y * lax.rsqrt(jnp.mean(y * y, axis=-1, keepdims=True) + EPS) * gfin_ref[...]
    return y


def _ffn_kernel(h_ref, x_ref, ada_ref, w1_ref, w3_ref, w2_ref, gfin_ref, o_ref, acc_ref, *, nf, final_norm):
    f = pl.program_id(1)

    @pl.when(f == 0)
    def _():
        acc_ref[...] = jnp.zeros_like(acc_ref)

    h = h_ref[...]
    act = jax.nn.silu(_dot(h, w1_ref[...])) * _dot(h, w3_ref[...])
    acc_ref[...] += _dot(act.astype(BF16), w2_ref[...])

    @pl.when(f == nf - 1)
    def _():
        o_ref[...] = _residual_out(x_ref[...], ada_ref[0], acc_ref[...], gfin_ref, final_norm)


def _ffn(h2, x, ada_l, w1, w3, w2, g_final, *, tm, cf, seq, final_norm):
    t, d = h2.shape
    dff = w1.shape[-1]
    nf = dff // cf
    tiles_per_seq = seq // tm
    kern = functools.partial(_ffn_kernel, nf=nf, final_norm=final_norm)
    return pl.pallas_call(
        kern,
        grid=(t // tm, nf),
        in_specs=[pl.BlockSpec((tm, d), lambda i, f: (i, 0)),
                  pl.BlockSpec((tm, d), lambda i, f: (i, 0)),
                  pl.BlockSpec((1, 6, d), lambda i, f: (i // tiles_per_seq, 0, 0)),
                  pl.BlockSpec((d, cf), lambda i, f: (0, f)),
                  pl.BlockSpec((d, cf), lambda i, f: (0, f)),
                  pl.BlockSpec((cf, d), lambda i, f: (f, 0)),
                  pl.BlockSpec((1, d), lambda i, f: (0, 0))],
        out_specs=pl.BlockSpec((tm, d), lambda i, f: (i, 0)),
        out_shape=jax.ShapeDtypeStruct((t, d), F32),
        scratch_shapes=[pltpu.VMEM((tm, d), F32)],
        compiler_params=_params(("arbitrary", "arbitrary")),
        name="ffn",
    )(h2, x, ada_l, w1, w3, w2, g_final)


def _moe_kernel(h_ref, rkc_ref, rkr_ref, gwr_ref, x_ref, ada_ref, w1_ref, w3_ref, w2_ref, gfin_ref, o_ref,
                *, n_exp, rows, final_norm):
    e = pl.program_id(1)
    tf = h_ref.shape[0]

    @pl.when(e == 0)
    def _():
        o_ref[...] = jnp.zeros_like(o_ref)

    lane = lax.broadcasted_iota(jnp.int32, (tf, LANES), 1)
    rk_col = jnp.sum(jnp.where(lane == e, rkc_ref[...], 0.0), axis=-1, keepdims=True)
    rk_row = rkr_ref[pl.ds(e, 1), :]
    gw_row = gwr_ref[pl.ds(e, 1), :]
    count = jnp.max(rk_row).astype(jnp.int32) + 1
    r_sub = lax.broadcasted_iota(jnp.int32, (rows, tf), 0).astype(F32)
    r_lane = lax.broadcasted_iota(jnp.int32, (tf, rows), 1).astype(F32)

    def chunk(c, carry):
        base = (c * rows).astype(F32)
        pick = rk_row == r_sub + base
        xc = _dot(pick.astype(BF16), h_ref[...]).astype(BF16)
        gate = jnp.sum(jnp.where(pick, gw_row, 0.0), axis=-1, keepdims=True)
        act = jax.nn.silu(_dot(xc, w1_ref[0])) * _dot(xc, w3_ref[0]) * gate
        yc = _dot(act.astype(BF16), w2_ref[0]).astype(BF16)
        o_ref[...] += _dot((rk_col == r_lane + base).astype(BF16), yc)
        return carry

    lax.fori_loop(0, (count + rows - 1) // rows, chunk, 0)

    @pl.when(e == n_exp - 1)
    def _():
        o_ref[...] = _residual_out(x_ref[...], ada_ref[0], o_ref[...], gfin_ref, final_norm)


def _moe(h2, rk_col, rk_row, gw_row, x, ada_l, w1, w3, w2, g_final, *, tf, rows, seq, final_norm):
    t, d = h2.shape
    n_exp, _, dff = w1.shape
    tiles_per_seq = seq // tf
    kern = functools.partial(_moe_kernel, n_exp=n_exp, rows=rows, final_norm=final_norm)
    once = pl.Buffered(1)
    return pl.pallas_call(
        kern,
        grid=(t // tf, n_exp),
        in_specs=[pl.BlockSpec((tf, d), lambda i, e: (i, 0), pipeline_mode=once),
                  pl.BlockSpec((tf, LANES), lambda i, e: (i, 0), pipeline_mode=once),
                  pl.BlockSpec((N_EXPERTS, tf), lambda i, e: (0, i), pipeline_mode=once),
                  pl.BlockSpec((N_EXPERTS, tf), lambda i, e: (0, i), pipeline_mode=once),
                  pl.BlockSpec((tf, d), lambda i, e: (i, 0), pipeline_mode=once),
                  pl.BlockSpec((1, 6, d), lambda i, e: (i // tiles_per_seq, 0, 0)),
                  pl.BlockSpec((1, d, dff), lambda i, e: (e, 0, 0)),
                  pl.BlockSpec((1, d, dff), lambda i, e: (e, 0, 0)),
                  pl.BlockSpec((1, dff, d), lambda i, e: (e, 0, 0)),
                  pl.BlockSpec((1, d), lambda i, e: (0, 0))],
        out_specs=pl.BlockSpec((tf, d), lambda i, e: (i, 0)),
        out_shape=jax.ShapeDtypeStruct((t, d), F32),
        compiler_params=_params(("arbitrary", "arbitrary")),
        name="moe",
    )(h2, rk_col, rk_row, gw_row, x, ada_l, w1, w3, w2, g_final)


def _tiles(seq, bsz):
    tm = min(512, seq)
    tq = min(1024, seq // 2)
    tf = min(1024, seq)
    return tm, tq, min(4, bsz), tf, 288


def kernel(x, c, w_ada, b_ada, g_mix, w_in, b_forget, b_gate, lam_re, lam_im, log_dt, b_re, b_im,
           c_re, c_im, d_skip, w_glu, b_glu, w_proj_att, w_proj_ssm, w_out, g_ffn, w1_dense, w3_dense,
           w2_dense, w_router, b_router, w1_moe, w3_moe, w2_moe, g_final):
    bsz, seq, d = x.shape
    depth = w_ada.shape[0]
    att_w = ATT_HEADS * ATT_HEAD_DIM
    ssm_w = w_glu.shape[-1]
    dff = w1_dense.shape[-1]
    tm, tq, bb, tf, rows = _tiles(seq, bsz)
    cf = dff // 2

    ada = _ada(c, w_ada, b_ada).reshape(depth, bsz, 6, d)
    gfin = g_final.reshape(1, d)

    for l in range(depth):
        o1, o2, o3 = 3 * att_w, 3 * att_w + ATT_HEADS, 3 * att_w + ATT_HEADS + ssm_w
        w = w_in[l]
        w_fg = jnp.pad(w[:, o1:o2], ((0, 0), (0, LANES - ATT_HEADS)))
        w_all = jnp.concatenate([w[:, :o1], w[:, o2:o3], w[:, o3:], w_fg], axis=1).astype(BF16)
        bf_pad = jnp.pad(b_forget[l], (0, LANES - ATT_HEADS)).reshape(1, LANES)

        qkv, u, gates, qaug, kaug = _inproj(x, ada[l], g_mix[l].reshape(1, d), w_all,
                                      b_gate[l].reshape(1, 2 * d), bf_pad,
                                      tm=tm, att_w=att_w, ssm_w=ssm_w)

        att = _attention(qkv, qaug, kaug, tq=tq)

        s5w = _s5_operators(lam_re[l], lam_im[l], log_dt[l], b_re[l], b_im[l], c_re[l], c_im[l])
        ys = _s5(u, (*s5w, d_skip[l].astype(F32).reshape(-1, 1, LANES)), bb=bb)

        moe = l % 2 == 1
        m = l // 2
        if moe:
            wr = jnp.pad(w_router[m], ((0, 0), (0, LANES - N_EXPERTS)))
            br = jnp.pad(b_router[m], (0, LANES - N_EXPERTS)).reshape(1, LANES)
        else:
            wr = jnp.zeros((d, LANES), F32)
            br = jnp.zeros((1, LANES), F32)
        wr_hi = wr.astype(BF16)
        wr_lo = (wr - wr_hi.astype(F32)).astype(BF16)
        mixed = _mix(att, ys, gates, x, ada[l],
                     w_proj_att[l].astype(BF16), w_glu[l].astype(BF16),
                     b_glu[l].reshape(1, ssm_w), w_proj_ssm[l].astype(BF16),
                     w_out[l].astype(BF16), g_ffn[l].reshape(1, d), wr_hi, wr_lo, br,
                     tm=tm, group=(tf // tm if moe else 0))
        x, h2 = mixed[0].reshape(bsz * seq, d), mixed[1].reshape(bsz * seq, d)
        last = l == depth - 1
        if moe:
            x = _moe(h2, mixed[2].reshape(bsz * seq, LANES), mixed[3], mixed[4], x, ada[l],
                     w1_moe[m].astype(BF16), w3_moe[m].astype(BF16), w2_moe[m].astype(BF16), gfin,
                     tf=tf, rows=rows, seq=seq, final_norm=last)
        else:
            x = _ffn(h2, x, ada[l], w1_dense[m].astype(BF16), w3_dense[m].astype(BF16),
                     w2_dense[m].astype(BF16), gfin, tm=tm, cf=cf, seq=seq, final_norm=last)
        x = x.reshape(bsz, seq, d)
    return x
```

```python
import functools
import math

import jax
import jax.numpy as jnp
from jax import lax
from jax.experimental import pallas as pl
from jax.experimental.pallas import tpu as pltpu

F32 = jnp.float32
BF16 = jnp.bfloat16
EPS = 1e-6

ATT_HEADS = 8
ATT_HEAD_DIM = 64
SSM_GROUP = 16
SSM_STATE = 64
N_EXPERTS = 8
LANES = 128
SUB = 8
LOG2E = math.log2(math.e)
F_OFF = 32
VMEM_LIMIT = 56 * 1024 * 1024


def _params(sem):
    return pltpu.CompilerParams(dimension_semantics=sem, vmem_limit_bytes=VMEM_LIMIT)


def _dot(a, b):
    return jnp.dot(a, b, preferred_element_type=F32)


def _ada_kernel(c_ref, w_ref, b_ref, o_ref):
    cond = jax.nn.silu(c_ref[...])
    o_ref[0] = jnp.dot(cond, w_ref[0], preferred_element_type=F32,
                       precision=lax.Precision.HIGHEST) + b_ref[0]


def _ada(c, w_ada, b_ada, tn=1536):
    depth, d, n = w_ada.shape
    bsz = c.shape[0]
    return pl.pallas_call(
        _ada_kernel,
        grid=(depth, n // tn),
        in_specs=[pl.BlockSpec((bsz, d), lambda l, j: (0, 0)),
                  pl.BlockSpec((1, d, tn), lambda l, j: (l, 0, j)),
                  pl.BlockSpec((1, 1, tn), lambda l, j: (l, 0, j))],
        out_specs=pl.BlockSpec((1, bsz, tn), lambda l, j: (l, 0, j)),
        out_shape=jax.ShapeDtypeStruct((depth, bsz, n), F32),
        compiler_params=_params(("arbitrary", "arbitrary")),
        name="ada",
    )(c, w_ada, b_ada.reshape(depth, 1, n))


def _modulated_norm(x, g, shift, scale):
    y = x * lax.rsqrt(jnp.mean(x * x, axis=-1, keepdims=True) + EPS) * g
    return y * (1.0 + scale) + shift


def _inproj_kernel(x_ref, ada_ref, g_ref, w_ref, bg_ref, bf_ref,
                   qkv_ref, u_ref, gate_ref, qa_ref, ka_ref, carry_ref, *, tm, att_w, ssm_w, d):
    j = pl.program_id(1)
    ada = ada_ref[0]
    h = _modulated_norm(x_ref[0], g_ref[...], ada[0:1], ada[1:2]).astype(BF16)

    cw = att_w
    qkv_ref[0, :, 0:cw] = (_dot(h, w_ref[:, 0:cw]) * (ATT_HEAD_DIM ** -0.5 * LOG2E)).astype(BF16)
    for c0 in range(cw, 3 * att_w, cw):
        qkv_ref[0, :, c0:c0 + cw] = _dot(h, w_ref[:, c0:c0 + cw]).astype(BF16)
    o_u = 3 * att_w
    u_ref[0] = _dot(h, w_ref[:, o_u:o_u + ssm_w])
    o_g = o_u + ssm_w
    for c0 in range(0, 2 * d, cw):
        z = _dot(h, w_ref[:, o_g + c0:o_g + c0 + cw]) + bg_ref[:, c0:c0 + cw]
        gate_ref[0, :, c0:c0 + cw] = jax.nn.sigmoid(z).astype(BF16)

    o_f = o_g + 2 * d
    logf = jax.nn.log_sigmoid(_dot(h, w_ref[:, o_f:o_f + LANES]) + bf_ref[...])
    row = lax.broadcasted_iota(jnp.int32, (tm, tm), 0)
    col = lax.broadcasted_iota(jnp.int32, (tm, tm), 1)
    tri = (row >= col).astype(BF16)
    hi = logf.astype(BF16)
    r1 = logf - hi.astype(F32)
    mid = r1.astype(BF16)
    lo = (r1 - mid.astype(F32)).astype(BF16)
    cs = _dot(tri, hi) + _dot(tri, mid) + _dot(tri, lo)

    @pl.when(j == 0)
    def _():
        carry_ref[...] = jnp.zeros_like(carry_ref)

    fcum = cs + carry_ref[0:1, :]
    carry_ref[...] = jnp.broadcast_to(fcum[tm - 1:tm, :], carry_ref.shape)

    lane = lax.broadcasted_iota(jnp.int32, (tm, LANES), 1)
    fl = jnp.where(lane < ATT_HEADS, fcum * LOG2E, 0.0)
    p0 = fl.astype(BF16).astype(F32)
    r0 = fl - p0
    p1 = r0.astype(BF16).astype(F32)
    p2 = (r0 - p1).astype(BF16).astype(F32)
    comb = p0 + pltpu.roll(p1, ATT_HEADS, 1) + pltpu.roll(p2, 2 * ATT_HEADS, 1)
    sel_lo = (lane < 3 * ATT_HEADS).astype(F32)
    sel_hi = ((lane >= F_OFF) & (lane < F_OFF + 3 * ATT_HEADS)).astype(F32)
    qa_ref[0] = (pltpu.roll(comb, F_OFF, 1) + sel_lo).astype(BF16)
    ka_ref[0] = (sel_hi - comb).astype(BF16)


def _inproj(x, ada_l, g, w_all, b_gate, b_forget_pad, *, tm, att_w, ssm_w):
    bsz, seq, d = x.shape
    n_all = w_all.shape[1]
    kern = functools.partial(_inproj_kernel, tm=tm, att_w=att_w, ssm_w=ssm_w, d=d)
    return pl.pallas_call(
        kern,
        grid=(bsz, seq // tm),
        in_specs=[pl.BlockSpec((1, tm, d), lambda b, j: (b, j, 0)),
                  pl.BlockSpec((1, 6, d), lambda b, j: (b, 0, 0)),
                  pl.BlockSpec((1, d), lambda b, j: (0, 0)),
                  pl.BlockSpec((d, n_all), lambda b, j: (0, 0)),
                  pl.BlockSpec((1, 2 * d), lambda b, j: (0, 0)),
                  pl.BlockSpec((1, LANES), lambda b, j: (0, 0))],
        out_specs=[pl.BlockSpec((1, tm, 3 * att_w), lambda b, j: (b, j, 0)),
                   pl.BlockSpec((1, tm, ssm_w), lambda b, j: (b, j, 0)),
                   pl.BlockSpec((1, tm, 2 * d), lambda b, j: (b, j, 0)),
                   pl.BlockSpec((1, tm, LANES), lambda b, j: (b, j, 0)),
                   pl.BlockSpec((1, tm, LANES), lambda b, j: (b, j, 0))],
        out_shape=[jax.ShapeDtypeStruct((bsz, seq, 3 * att_w), BF16),
                   jax.ShapeDtypeStruct((bsz, seq, ssm_w), F32),
                   jax.ShapeDtypeStruct((bsz, seq, 2 * d), BF16),
                   jax.ShapeDtypeStruct((bsz, seq, LANES), BF16),
                   jax.ShapeDtypeStruct((bsz, seq, LANES), BF16)],
        scratch_shapes=[pltpu.VMEM((8, LANES), F32)],
        compiler_params=_params(("arbitrary", "arbitrary")),
        name="inproj",
    )(x, ada_l, g, w_all, b_gate, b_forget_pad)


def _attn_kernel(q_ref, qa_ref, k_ref, ka_ref, v_ref, o_ref, *, tq):
    hp = pl.program_id(1)
    i = pl.program_id(2)
    half = tq // 2
    lane = lax.broadcasted_iota(jnp.int32, (tq, LANES), 1)
    first = lane < ATT_HEAD_DIM
    q2 = q_ref[0]
    qa = qa_ref[0]
    zero = jnp.zeros_like(q2)
    heads = []
    for hh in range(2):
        qsel = jnp.where(first, q2, zero) if hh == 0 else jnp.where(first, zero, q2)
        asel = jnp.where((lane & (ATT_HEADS - 1)) == 2 * hp + hh, qa, zero)
        heads.append(jnp.concatenate([qsel, asel], axis=1))

    def kv(start, size):
        kj = jnp.concatenate([k_ref[0, pl.ds(start, size), :], ka_ref[0, pl.ds(start, size), :]], axis=1)
        v2 = v_ref[0, pl.ds(start, size), :]
        one = jnp.ones_like(v2)
        lk = lax.broadcasted_iota(jnp.int32, v2.shape, 1) < ATT_HEAD_DIM
        return kj, (jnp.where(lk, v2, one), jnp.where(lk, one, v2))

    def update(qh, kj, vh, m, acc, mask):
        s = lax.dot_general(qh, kj, (((1,), (1,)), ((), ())), preferred_element_type=F32)
        if mask is not None:
            s = jnp.where(mask, s, -jnp.inf)
        m_new = jnp.maximum(m, jnp.max(s, axis=-1, keepdims=True))
        p = jnp.exp2(s - m_new).astype(BF16)
        acc = jnp.exp2(m - m_new) * acc + _dot(p, vh)
        return m_new, acc

    def body(j, carry):
        kj, vs = kv(pl.multiple_of(j * tq, tq), tq)
        out = []
        for hh in range(2):
            out.extend(update(heads[hh], kj, vs[hh], carry[2 * hh], carry[2 * hh + 1], None))
        return tuple(out)

    init = []
    for hh in range(2):
        init += [jnp.full((tq, 1), -jnp.inf, F32), jnp.zeros((tq, LANES), F32)]
    carry = lax.fori_loop(0, i, body, tuple(init))

    base = pl.multiple_of(i * tq, tq)
    r_top = lax.broadcasted_iota(jnp.int32, (half, half), 0)
    c_top = lax.broadcasted_iota(jnp.int32, (half, half), 1)
    r_bot = lax.broadcasted_iota(jnp.int32, (half, tq), 0)
    c_bot = lax.broadcasted_iota(jnp.int32, (half, tq), 1)
    k_top, v_top = kv(base, half)
    k_all, v_all = kv(base, tq)
    outs = []
    for hh in range(2):
        m, acc = carry[2 * hh], carry[2 * hh + 1]
        qh = heads[hh]
        _, acc_t = update(qh[:half], k_top, v_top[hh], m[:half], acc[:half], r_top >= c_top)
        _, acc_b = update(qh[half:], k_all, v_all[hh], m[half:], acc[half:], r_bot + half >= c_bot)
        acc = jnp.concatenate([acc_t, acc_b], axis=0)
        outs.append(acc / pltpu.roll(acc, ATT_HEAD_DIM, 1))
    o_ref[0] = jnp.where(first, outs[0], outs[1]).astype(BF16)


def _attention(qkv, qaug, kaug, *, tq):
    bsz, seq, w3 = qkv.shape
    att_w = w3 // 3
    npair = att_w // LANES
    kern = functools.partial(_attn_kernel, tq=tq)
    return pl.pallas_call(
        kern,
        grid=(bsz, npair, seq // tq),
        in_specs=[pl.BlockSpec((1, tq, LANES), lambda b, p, i: (b, i, p)),
                  pl.BlockSpec((1, tq, LANES), lambda b, p, i: (b, i, 0)),
                  pl.BlockSpec((1, seq, LANES), lambda b, p, i: (b, 0, npair + p)),
                  pl.BlockSpec((1, seq, LANES), lambda b, p, i: (b, 0, 0)),
                  pl.BlockSpec((1, seq, LANES), lambda b, p, i: (b, 0, 2 * npair + p))],
        out_specs=pl.BlockSpec((1, tq, LANES), lambda b, p, i: (b, i, p)),
        out_shape=jax.ShapeDtypeStruct((bsz, seq, att_w), BF16),
        compiler_params=_params(("arbitrary", "arbitrary", "arbitrary")),
        name="fox_attention",
    )(qkv, qaug, qkv, kaug, qkv)


def _split(x):
    hi = x.astype(BF16)
    return hi, (x - hi.astype(F32)).astype(BF16)


def _dot_nt3(a, b):
    dims = (((1,), (1,)), ((), ()))
    ah, al = _split(a)
    bh, bl = _split(b)
    nt = lambda u, v: lax.dot_general(u, v, dims, preferred_element_type=F32)
    return nt(ah, bh) + nt(ah, bl) + nt(al, bh)


def _dot_rep3(x, rep):
    p0 = x.astype(BF16)
    r0 = x - p0.astype(F32)
    p1 = r0.astype(BF16)
    p2 = (r0 - p1.astype(F32)).astype(BF16)
    return _dot(p0, rep) + _dot(p1, rep) + _dot(p2, rep)


def _s5_operator_kernel(lr_ref, li_ref, dt_ref, br_ref, bi_ref, cr_ref, ci_ref,
                        t_ref, b_ref, ct_ref, a1_ref, a2_ref, *, hc, p):
    gl = LANES // hc
    half = gl * p
    lr, li, dt = lr_ref[0], li_ref[0], jnp.exp(dt_ref[0])
    br, bi, cr, ci = br_ref[0], bi_ref[0], cr_ref[0], ci_ref[0]
    mag = jnp.exp(lr * dt)
    ang = li * dt
    a_re, a_im = mag * jnp.cos(ang), mag * jnp.sin(ang)
    n_re, n_im = a_re - 1.0, a_im
    den = lr * lr + li * li
    k_re = (n_re * lr + n_im * li) / den
    k_im = (n_im * lr - n_re * li) / den
    bb_re = k_re * br - k_im * bi
    bb_im = k_re * bi + k_im * br

    def power(t):
        m = jnp.exp(lr * dt * t)
        return m * jnp.cos(ang * t), m * jnp.sin(ang * t)

    pw = [power(float(t)) for t in range(SUB + 1)]

    rep = (lax.broadcasted_iota(jnp.int32, (p, half), 1) % p
           == lax.broadcasted_iota(jnp.int32, (p, half), 0)).astype(BF16)
    own = (lax.broadcasted_iota(jnp.int32, (LANES, half), 0) // hc
           == lax.broadcasted_iota(jnp.int32, (LANES, half), 1) // p)

    def expand(x):
        return jnp.where(own, _dot(x.astype(BF16), rep), 0.0).astype(BF16)

    same = (lax.broadcasted_iota(jnp.int32, (LANES, LANES), 0) // hc
            == lax.broadcasted_iota(jnp.int32, (LANES, LANES), 1) // hc)
    zero = jnp.zeros((LANES, LANES), BF16)
    kern = []
    for t in range(SUB):
        pr, pi = pw[t]
        wr, wi = cr * pr - ci * pi, cr * pi + ci * pr
        k = _dot_nt3(bb_re, wr) - _dot_nt3(bb_im, wi)
        kern.append(jnp.where(same, k, 0.0).astype(BF16))
    for j in range(SUB):
        for i in range(SUB):
            t_ref[0, j * LANES:(j + 1) * LANES, i * LANES:(i + 1) * LANES] = kern[i - j] if i >= j else zero
        pr, pi = pw[SUB - 1 - j]
        b_ref[0, j * LANES:(j + 1) * LANES, 0:half] = expand(pr * bb_re - pi * bb_im)
        b_ref[0, j * LANES:(j + 1) * LANES, half:2 * half] = expand(pr * bb_im + pi * bb_re)
        pr, pi = pw[j + 1]
        ct_ref[0, j * LANES:(j + 1) * LANES, 0:half] = expand(cr * pr - ci * pi)
        ct_ref[0, j * LANES:(j + 1) * LANES, half:2 * half] = expand(-(cr * pi + ci * pr))
    pr, pi = pw[SUB]
    along = lambda x: jnp.sum(jnp.where(own, _dot_rep3(x, rep), 0.0), axis=0, keepdims=True) * (1.0 / hc)
    ac_re, ac_im = along(pr), along(pi)
    a1_ref[0] = jnp.concatenate([ac_re, ac_re], axis=1)
    a2_ref[0] = jnp.concatenate([-ac_im, ac_im], axis=1)


def _s5_operators(lam_re, lam_im, log_dt, b_re, b_im, c_re, c_im):
    g, p = lam_re.shape
    hc = b_re.shape[-1]
    nt = g * hc // LANES
    ch = SUB * LANES
    ns = 2 * (LANES // hc) * p
    rows = lambda v: jnp.broadcast_to(v[:, None, :], (g, hc, v.shape[-1])).reshape(nt, LANES, v.shape[-1])
    compact = [rows(lam_re.astype(F32)), rows(lam_im.astype(F32)),
               rows(jnp.broadcast_to(log_dt.astype(F32)[:, None], (g, p))),
               b_re.astype(F32).transpose(0, 2, 1).reshape(nt, LANES, p),
               b_im.astype(F32).transpose(0, 2, 1).reshape(nt, LANES, p),
               c_re.astype(F32).reshape(nt, LANES, p), c_im.astype(F32).reshape(nt, LANES, p)]
    kern = functools.partial(_s5_operator_kernel, hc=hc, p=p)
    return pl.pallas_call(
        kern,
        grid=(nt,),
        in_specs=[pl.BlockSpec((1, LANES, p), lambda t: (t, 0, 0))] * 7,
        out_specs=[pl.BlockSpec((1, ch, ch), lambda t: (t, 0, 0)),
                   pl.BlockSpec((1, ch, ns), lambda t: (t, 0, 0)),
                   pl.BlockSpec((1, ch, ns), lambda t: (t, 0, 0)),
                   pl.BlockSpec((1, 1, ns), lambda t: (t, 0, 0)),
                   pl.BlockSpec((1, 1, ns), lambda t: (t, 0, 0))],
        out_shape=[jax.ShapeDtypeStruct((nt, ch, ch), BF16), jax.ShapeDtypeStruct((nt, ch, ns), BF16),
                   jax.ShapeDtypeStruct((nt, ch, ns), BF16), jax.ShapeDtypeStruct((nt, 1, ns), F32),
                   jax.ShapeDtypeStruct((nt, 1, ns), F32)],
        compiler_params=_params(("arbitrary",)),
        name="s5_operators",
    )(*compact)


def _s5_kernel(u_ref, t_ref, b_ref, ct_ref, a1_ref, a2_ref, d_ref, y_ref, uf_ref, s_ref, *, bb, nk):
    nc = s_ref.shape[0]
    for b in range(bb):
        for j in range(SUB):
            uf_ref[b * nk:(b + 1) * nk, j * LANES:(j + 1) * LANES] = (
                u_ref[b, pl.ds(j, nk, stride=SUB), :].astype(BF16))
    for b in range(bb):
        rows = slice(b * nk, (b + 1) * nk)
        s_loc = _dot(uf_ref[rows, :], b_ref[0])
        for c in range(nc):
            s_ref[c, rows, :] = s_loc[:, c * LANES:(c + 1) * LANES]
    a1 = a1_ref[0]
    a2 = a2_ref[0]

    def carry_step(k, state):
        rows = pl.ds(k, bb, stride=nk)
        loc = jnp.concatenate([s_ref[c, rows, :] for c in range(nc)], axis=1)
        for c in range(nc):
            s_ref[c, rows, :] = state[:, c * LANES:(c + 1) * LANES]
        return a1 * state + a2 * pltpu.roll(state, nc * LANES // 2, 1) + loc

    lax.fori_loop(0, nk, carry_step, jnp.zeros((bb, nc * LANES), F32))
    d = d_ref[0]
    for b in range(bb):
        rows = slice(b * nk, (b + 1) * nk)
        s_prev = jnp.concatenate([s_ref[c, rows, :] for c in range(nc)], axis=1).astype(BF16)
        y = _dot(uf_ref[rows, :], t_ref[0]) + lax.dot_general(
            s_prev, ct_ref[0], (((1,), (1,)), ((), ())), preferred_element_type=F32)
        for i in range(SUB):
            yi = y[:, i * LANES:(i + 1) * LANES] + d * u_ref[b, pl.ds(i, nk, stride=SUB), :]
            y_ref[b, pl.ds(i, nk, stride=SUB), :] = jax.nn.gelu(yi)


def _s5(u, weights, *, bb):
    bsz, seq, w = u.shape
    t_intra, b_state, c_state, a1, a2, dvec = weights
    nt = t_intra.shape[0]
    nk = seq // SUB
    ch = SUB * LANES
    ns = b_state.shape[-1]
    kern = functools.partial(_s5_kernel, bb=bb, nk=nk)
    return pl.pallas_call(
        kern,
        grid=(nt, bsz // bb),
        in_specs=[pl.BlockSpec((bb, seq, LANES), lambda t, i: (i, 0, t)),
                  pl.BlockSpec((1, ch, ch), lambda t, i: (t, 0, 0)),
                  pl.BlockSpec((1, ch, ns), lambda t, i: (t, 0, 0)),
                  pl.BlockSpec((1, ns, ch), lambda t, i: (t, 0, 0)),
                  pl.BlockSpec((1, 1, ns), lambda t, i: (t, 0, 0)),
                  pl.BlockSpec((1, 1, ns), lambda t, i: (t, 0, 0)),
                  pl.BlockSpec((1, 1, LANES), lambda t, i: (t, 0, 0))],
        out_specs=pl.BlockSpec((bb, seq, LANES), lambda t, i: (i, 0, t)),
        out_shape=jax.ShapeDtypeStruct((bsz, seq, w), F32),
        scratch_shapes=[pltpu.VMEM((bb * nk, ch), BF16), pltpu.VMEM((ns // LANES, bb * nk, LANES), F32)],
        compiler_params=_params(("arbitrary", "arbitrary")),
        name="s5_chunked",
    )(u, t_intra, b_state, c_state, a1, a2, dvec)


def _route(logits):
    lane = lax.broadcasted_iota(jnp.int32, logits.shape, 1)
    valid = lane < N_EXPERTS
    neg = jnp.float32(-jnp.inf)
    lg = jnp.where(valid, logits, neg)
    m1 = jnp.max(lg, axis=-1, keepdims=True)
    i1 = jnp.min(jnp.where(lg == m1, lane, LANES), axis=-1, keepdims=True)
    lg2 = jnp.where(lane == i1, neg, lg)
    m2 = jnp.max(lg2, axis=-1, keepdims=True)
    i2 = jnp.min(jnp.where(lg2 == m2, lane, LANES), axis=-1, keepdims=True)
    e2 = jnp.exp(m2 - m1)
    den = 1.0 + e2
    gates = jnp.where(lane == i1, 1.0 / den, 0.0) + jnp.where(lane == i2, e2 / den, 0.0)
    return gates, (lane == i1) | (lane == i2)


def _mix_kernel(att_ref, ys_ref, gate_ref, x_ref, ada_ref, wpa_ref, wglu_ref, bglu_ref, wps_ref,
                wout_ref, gffn_ref, wrh_ref, wrl_ref, br_ref, xo_ref, h2_ref, *route_refs, d, tm, group):
    ada = ada_ref[0]
    ya = _dot(att_ref[0], wpa_ref[...])
    ys = ys_ref[0]
    glu = ys * jax.nn.sigmoid(_dot(ys.astype(BF16), wglu_ref[...]) + bglu_ref[...])
    yp = _dot(glu.astype(BF16), wps_ref[...])
    mixed = gate_ref[0, :, 0:d].astype(F32) * ya + gate_ref[0, :, d:2 * d].astype(F32) * yp
    x_new = x_ref[0] + ada[2:3] * _dot(mixed.astype(BF16), wout_ref[...])
    xo_ref[0] = x_new
    h2 = _modulated_norm(x_new, gffn_ref[...], ada[3:4], ada[4:5])
    h2b = h2.astype(BF16)
    h2_ref[0] = h2b
    if not route_refs:
        return
    rkr_ref, gwr_ref, cnt_ref = route_refs
    h2l = (h2 - h2b.astype(F32)).astype(BF16)
    logits = (_dot(h2b, wrh_ref[...]) + _dot(h2b, wrl_ref[...]) + _dot(h2l, wrh_ref[...])) + br_ref[...]
    gates, mask = _route(logits)

    @pl.when(pl.program_id(1) % group == 0)
    def _():
        cnt_ref[...] = jnp.zeros_like(cnt_ref)

    row = lax.broadcasted_iota(jnp.int32, (tm, tm), 0)
    col = lax.broadcasted_iota(jnp.int32, (tm, tm), 1)
    maskf = mask.astype(F32)
    before = _dot((row > col).astype(BF16), maskf.astype(BF16)) + cnt_ref[0:1, :]
    rank = jnp.where(mask, before, -1.0)
    cnt_ref[...] = jnp.broadcast_to(before[tm - 1:tm, :] + maskf[tm - 1:tm, :], cnt_ref.shape)
    rkr_ref[...] = jnp.transpose(rank)[0:N_EXPERTS, :]
    gwr_ref[...] = jnp.transpose(gates)[0:N_EXPERTS, :]


def _mix(att, ys, gates, x, ada_l, wpa, wglu, bglu, wps, wout, gffn, wr_hi, wr_lo, b_router, *, tm, group):
    bsz, seq, d = x.shape
    aw = att.shape[-1]
    sw = ys.shape[-1]
    nj = seq // tm
    kern = functools.partial(_mix_kernel, d=d, tm=tm, group=group)
    const = lambda b, j: (0, 0)
    tile = lambda b, j: (b, j, 0)
    out_specs = [pl.BlockSpec((1, tm, d), tile), pl.BlockSpec((1, tm, d), tile)]
    out_shape = [jax.ShapeDtypeStruct((bsz, seq, d), F32), jax.ShapeDtypeStruct((bsz, seq, d), BF16)]
    scratch = []
    if group:
        flat = lambda b, j: (0, b * nj + j)
        out_specs += [pl.BlockSpec((N_EXPERTS, tm), flat), pl.BlockSpec((N_EXPERTS, tm), flat)]
        out_shape += [jax.ShapeDtypeStruct((N_EXPERTS, bsz * seq), F32),
                      jax.ShapeDtypeStruct((N_EXPERTS, bsz * seq), F32)]
        scratch = [pltpu.VMEM((SUB, LANES), F32)]
    return pl.pallas_call(
        kern,
        grid=(bsz, nj),
        in_specs=[pl.BlockSpec((1, tm, aw), tile),
                  pl.BlockSpec((1, tm, sw), tile),
                  pl.BlockSpec((1, tm, 2 * d), tile),
                  pl.BlockSpec((1, tm, d), tile),
                  pl.BlockSpec((1, 6, d), lambda b, j: (b, 0, 0)),
                  pl.BlockSpec((aw, d), const),
                  pl.BlockSpec((sw, sw), const),
                  pl.BlockSpec((1, sw), const),
                  pl.BlockSpec((sw, d), const),
                  pl.BlockSpec((d, d), const),
                  pl.BlockSpec((1, d), const),
                  pl.BlockSpec((d, LANES), const),
                  pl.BlockSpec((d, LANES), const),
                  pl.BlockSpec((1, LANES), const)],
        out_specs=out_specs,
        out_shape=out_shape,
        scratch_shapes=scratch,
        compiler_params=_params(("arbitrary", "arbitrary")),
        name="mix",
    )(att, ys, gates, x, ada_l, wpa, wglu, bglu, wps, wout, gffn, wr_hi, wr_lo, b_router)


def _residual_out(x, ada, acc, gfin_ref, final_norm):
    y = x + ada[5:6] * acc
    if final_norm:
        y = y * lax.rsqrt(jnp.mean(y * y, axis=-1, keepdims=True) + EPS) * gfin_ref[...]
    return y


def _ffn_kernel(h_ref, x_ref, ada_ref, w1_ref, w3_ref, w2_ref, gfin_ref, o_ref, acc_ref, *, nf, final_norm):
    f = pl.program_id(1)

    @pl.when(f == 0)
    def _():
        acc_ref[...] = jnp.zeros_like(acc_ref)

    h = h_ref[...]
    act = jax.nn.silu(_dot(h, w1_ref[...])) * _dot(h, w3_ref[...])
    acc_ref[...] += _dot(act.astype(BF16), w2_ref[...])

    @pl.when(f == nf - 1)
    def _():
        o_ref[...] = _residual_out(x_ref[...], ada_ref[0], acc_ref[...], gfin_ref, final_norm)


def _ffn(h2, x, ada_l, w1, w3, w2, g_final, *, tm, cf, seq, final_norm):
    t, d = h2.shape
    dff = w1.shape[-1]
    nf = dff // cf
    tiles_per_seq = seq // tm
    kern = functools.partial(_ffn_kernel, nf=nf, final_norm=final_norm)
    return pl.pallas_call(
        kern,
        grid=(t // tm, nf),
        in_specs=[pl.BlockSpec((tm, d), lambda i, f: (i, 0)),
                  pl.BlockSpec((tm, d), lambda i, f: (i, 0)),
                  pl.BlockSpec((1, 6, d), lambda i, f: (i // tiles_per_seq, 0, 0)),
                  pl.BlockSpec((d, cf), lambda i, f: (0, f)),
                  pl.BlockSpec((d, cf), lambda i, f: (0, f)),
                  pl.BlockSpec((cf, d), lambda i, f: (f, 0)),
                  pl.BlockSpec((1, d), lambda i, f: (0, 0))],
        out_specs=pl.BlockSpec((tm, d), lambda i, f: (i, 0)),
        out_shape=jax.ShapeDtypeStruct((t, d), F32),
        scratch_shapes=[pltpu.VMEM((tm, d), F32)],
        compiler_params=_params(("arbitrary", "arbitrary")),
        name="ffn",
    )(h2, x, ada_l, w1, w3, w2, g_final)


def _moe_kernel(h_ref, rkr_ref, gwr_ref, x_ref, ada_ref, w1_ref, w3_ref, w2_ref, gfin_ref, o_ref,
                *, n_exp, rows, final_norm):
    e = pl.program_id(1)
    tf = h_ref.shape[0]

    @pl.when(e == 0)
    def _():
        o_ref[...] = jnp.zeros_like(o_ref)

    rk_row = rkr_ref[pl.ds(e, 1), :]
    gw_row = gwr_ref[pl.ds(e, 1), :]
    count = jnp.max(rk_row).astype(jnp.int32) + 1

    r_sub = lax.broadcasted_iota(jnp.int32, (rows, tf), 0).astype(F32)

    def chunk(c, carry):
        pick = rk_row == r_sub + (c * rows).astype(F32)
        pickb = pick.astype(BF16)
        xc = _dot(pickb, h_ref[...]).astype(BF16)
        gate = jnp.sum(jnp.where(pick, gw_row, 0.0), axis=-1, keepdims=True)
        act = jax.nn.silu(_dot(xc, w1_ref[0])) * _dot(xc, w3_ref[0]) * gate
        yc = _dot(act.astype(BF16), w2_ref[0]).astype(BF16)
        o_ref[...] += lax.dot_general(pickb, yc, (((0,), (0,)), ((), ())), preferred_element_type=F32)
        return carry

    lax.fori_loop(0, (count + rows - 1) // rows, chunk, 0)

    @pl.when(e == n_exp - 1)
    def _():
        o_ref[...] = _residual_out(x_ref[...], ada_ref[0], o_ref[...], gfin_ref, final_norm)


def _moe(h2, rk_row, gw_row, x, ada_l, w1, w3, w2, g_final, *, tf, rows, seq, final_norm):
    t, d = h2.shape
    n_exp, _, dff = w1.shape
    tiles_per_seq = seq // tf
    kern = functools.partial(_moe_kernel, n_exp=n_exp, rows=rows, final_norm=final_norm)
    once = pl.Buffered(1)
    return pl.pallas_call(
        kern,
        grid=(t // tf, n_exp),
        in_specs=[pl.BlockSpec((tf, d), lambda i, e: (i, 0), pipeline_mode=once),
                  pl.BlockSpec((N_EXPERTS, tf), lambda i, e: (0, i), pipeline_mode=once),
                  pl.BlockSpec((N_EXPERTS, tf), lambda i, e: (0, i), pipeline_mode=once),
                  pl.BlockSpec((tf, d), lambda i, e: (i, 0), pipeline_mode=once),
                  pl.BlockSpec((1, 6, d), lambda i, e: (i // tiles_per_seq, 0, 0)),
                  pl.BlockSpec((1, d, dff), lambda i, e: (e, 0, 0)),
                  pl.BlockSpec((1, d, dff), lambda i, e: (e, 0, 0)),
                  pl.BlockSpec((1, dff, d), lambda i, e: (e, 0, 0)),
                  pl.BlockSpec((1, d), lambda i, e: (0, 0))],
        out_specs=pl.BlockSpec((tf, d), lambda i, e: (i, 0)),
        out_shape=jax.ShapeDtypeStruct((t, d), F32),
        compiler_params=_params(("arbitrary", "arbitrary")),
        name="moe",
    )(h2, rk_row, gw_row, x, ada_l, w1, w3, w2, g_final)


def _tiles(seq, bsz):
    tm = min(512, seq)
    tq = min(1024, seq // 2)
    tf = min(1024, seq)
    return tm, tq, min(4, bsz), tf, 144


def kernel(x, c, w_ada, b_ada, g_mix, w_in, b_forget, b_gate, lam_re, lam_im, log_dt, b_re, b_im,
           c_re, c_im, d_skip, w_glu, b_glu, w_proj_att, w_proj_ssm, w_out, g_ffn, w1_dense, w3_dense,
           w2_dense, w_router, b_router, w1_moe, w3_moe, w2_moe, g_final):
    bsz, seq, d = x.shape
    depth = w_ada.shape[0]
    att_w = ATT_HEADS * ATT_HEAD_DIM
    ssm_w = w_glu.shape[-1]
    dff = w1_dense.shape[-1]
    tm, tq, bb, tf, rows = _tiles(seq, bsz)
    cf = dff // 2

    ada = _ada(c, w_ada, b_ada).reshape(depth, bsz, 6, d)
    gfin = g_final.reshape(1, d)

    for l in range(depth):
        o1, o2, o3 = 3 * att_w, 3 * att_w + ATT_HEADS, 3 * att_w + ATT_HEADS + ssm_w
        w = w_in[l]
        w_fg = jnp.pad(w[:, o1:o2], ((0, 0), (0, LANES - ATT_HEADS)))
        w_all = jnp.concatenate([w[:, :o1], w[:, o2:o3], w[:, o3:], w_fg], axis=1).astype(BF16)
        bf_pad = jnp.pad(b_forget[l], (0, LANES - ATT_HEADS)).reshape(1, LANES)

        qkv, u, gates, qaug, kaug = _inproj(x, ada[l], g_mix[l].reshape(1, d), w_all,
                                      b_gate[l].reshape(1, 2 * d), bf_pad,
                                      tm=tm, att_w=att_w, ssm_w=ssm_w)

        att = _attention(qkv, qaug, kaug, tq=tq)

        s5w = _s5_operators(lam_re[l], lam_im[l], log_dt[l], b_re[l], b_im[l], c_re[l], c_im[l])
        ys = _s5(u, (*s5w, d_skip[l].astype(F32).reshape(-1, 1, LANES)), bb=bb)

        moe = l % 2 == 1
        m = l // 2
        if moe:
            wr = jnp.pad(w_router[m], ((0, 0), (0, LANES - N_EXPERTS)))
            br = jnp.pad(b_router[m], (0, LANES - N_EXPERTS)).reshape(1, LANES)
        else:
            wr = jnp.zeros((d, LANES), F32)
            br = jnp.zeros((1, LANES), F32)
        wr_hi = wr.astype(BF16)
        wr_lo = (wr - wr_hi.astype(F32)).astype(BF16)
        mixed = _mix(att, ys, gates, x, ada[l],
                     w_proj_att[l].astype(BF16), w_glu[l].astype(BF16),
                     b_glu[l].reshape(1, ssm_w), w_proj_ssm[l].astype(BF16),
                     w_out[l].astype(BF16), g_ffn[l].reshape(1, d), wr_hi, wr_lo, br,
                     tm=tm, group=(tf // tm if moe else 0))
        x, h2 = mixed[0].reshape(bsz * seq, d), mixed[1].reshape(bsz * seq, d)
        last = l == depth - 1
        if moe:
            x = _moe(h2, mixed[2], mixed[3], x, ada[l],
                     w1_moe[m].astype(BF16), w3_moe[m].astype(BF16), w2_moe[m].astype(BF16), gfin,
                     tf=tf, rows=rows, seq=seq, final_norm=last)
        else:
            x = _ffn(h2, x, ada[l], w1_dense[m].astype(BF16), w3_dense[m].astype(BF16),
                     w2_dense[m].astype(BF16), gfin, tm=tf, cf=cf, seq=seq, final_norm=last)
        x = x.reshape(bsz, seq, d)
    return x
```

```python
import functools
import math

import jax
import jax.numpy as jnp
from jax import lax
from jax.experimental import pallas as pl
from jax.experimental.pallas import tpu as pltpu

F32 = jnp.float32
BF16 = jnp.bfloat16
EPS = 1e-6

ATT_HEADS = 8
ATT_HEAD_DIM = 64
SSM_GROUP = 16
SSM_STATE = 64
N_EXPERTS = 8
LANES = 128
SUB = 8
LOG2E = math.log2(math.e)
F_OFF = 32
VMEM_LIMIT = 56 * 1024 * 1024


def _params(sem):
    return pltpu.CompilerParams(dimension_semantics=sem, vmem_limit_bytes=VMEM_LIMIT)


def _dot(a, b):
    return jnp.dot(a, b, preferred_element_type=F32)


def _ada_kernel(c_ref, w_ref, b_ref, o_ref):
    cond = jax.nn.silu(c_ref[...])
    o_ref[0] = jnp.dot(cond, w_ref[0], preferred_element_type=F32,
                       precision=lax.Precision.HIGHEST) + b_ref[0]


def _ada(c, w_ada, b_ada, tn=1536):
    depth, d, n = w_ada.shape
    bsz = c.shape[0]
    return pl.pallas_call(
        _ada_kernel,
        grid=(depth, n // tn),
        in_specs=[pl.BlockSpec((bsz, d), lambda l, j: (0, 0)),
                  pl.BlockSpec((1, d, tn), lambda l, j: (l, 0, j)),
                  pl.BlockSpec((1, 1, tn), lambda l, j: (l, 0, j))],
        out_specs=pl.BlockSpec((1, bsz, tn), lambda l, j: (l, 0, j)),
        out_shape=jax.ShapeDtypeStruct((depth, bsz, n), F32),
        compiler_params=_params(("arbitrary", "arbitrary")),
        name="ada",
    )(c, w_ada, b_ada.reshape(depth, 1, n))


def _modulated_norm(x, g, shift, scale):
    y = x * lax.rsqrt(jnp.mean(x * x, axis=-1, keepdims=True) + EPS) * g
    return y * (1.0 + scale) + shift


def _inproj_kernel(x_ref, ada_ref, g_ref, w_ref, bg_ref, bf_ref,
                   qkv_ref, u_ref, gate_ref, qa_ref, ka_ref, carry_ref, *, tm, att_w, ssm_w, d):
    j = pl.program_id(1)
    ada = ada_ref[0]
    h = _modulated_norm(x_ref[0], g_ref[...], ada[0:1], ada[1:2]).astype(BF16)

    cw = att_w
    qkv_ref[0, :, 0:cw] = (_dot(h, w_ref[:, 0:cw]) * (ATT_HEAD_DIM ** -0.5 * LOG2E)).astype(BF16)
    for c0 in range(cw, 3 * att_w, cw):
        qkv_ref[0, :, c0:c0 + cw] = _dot(h, w_ref[:, c0:c0 + cw]).astype(BF16)
    o_u = 3 * att_w
    u_ref[0] = _dot(h, w_ref[:, o_u:o_u + ssm_w])
    o_g = o_u + ssm_w
    for c0 in range(0, 2 * d, cw):
        z = _dot(h, w_ref[:, o_g + c0:o_g + c0 + cw]) + bg_ref[:, c0:c0 + cw]
        gate_ref[0, :, c0:c0 + cw] = jax.nn.sigmoid(z).astype(BF16)

    o_f = o_g + 2 * d
    logf = jax.nn.log_sigmoid(_dot(h, w_ref[:, o_f:o_f + LANES]) + bf_ref[...])
    row = lax.broadcasted_iota(jnp.int32, (tm, tm), 0)
    col = lax.broadcasted_iota(jnp.int32, (tm, tm), 1)
    tri = (row >= col).astype(BF16)
    hi = logf.astype(BF16)
    r1 = logf - hi.astype(F32)
    mid = r1.astype(BF16)
    lo = (r1 - mid.astype(F32)).astype(BF16)
    cs = _dot(tri, hi) + _dot(tri, mid) + _dot(tri, lo)

    @pl.when(j == 0)
    def _():
        carry_ref[...] = jnp.zeros_like(carry_ref)

    fcum = cs + carry_ref[0:1, :]
    carry_ref[...] = jnp.broadcast_to(fcum[tm - 1:tm, :], carry_ref.shape)

    lane = lax.broadcasted_iota(jnp.int32, (tm, LANES), 1)
    fl = jnp.where(lane < ATT_HEADS, fcum * LOG2E, 0.0)
    p0 = fl.astype(BF16).astype(F32)
    r0 = fl - p0
    p1 = r0.astype(BF16).astype(F32)
    p2 = (r0 - p1).astype(BF16).astype(F32)
    comb = p0 + pltpu.roll(p1, ATT_HEADS, 1) + pltpu.roll(p2, 2 * ATT_HEADS, 1)
    sel_lo = (lane < 3 * ATT_HEADS).astype(F32)
    sel_hi = ((lane >= F_OFF) & (lane < F_OFF + 3 * ATT_HEADS)).astype(F32)
    qa_ref[0] = (pltpu.roll(comb, F_OFF, 1) + sel_lo).astype(BF16)
    ka_ref[0] = (sel_hi - comb).astype(BF16)


def _inproj(x, ada_l, g, w_all, b_gate, b_forget_pad, *, tm, att_w, ssm_w):
    bsz, seq, d = x.shape
    n_all = w_all.shape[1]
    kern = functools.partial(_inproj_kernel, tm=tm, att_w=att_w, ssm_w=ssm_w, d=d)
    return pl.pallas_call(
        kern,
        grid=(bsz, seq // tm),
        in_specs=[pl.BlockSpec((1, tm, d), lambda b, j: (b, j, 0)),
                  pl.BlockSpec((1, 6, d), lambda b, j: (b, 0, 0)),
                  pl.BlockSpec((1, d), lambda b, j: (0, 0)),
                  pl.BlockSpec((d, n_all), lambda b, j: (0, 0)),
                  pl.BlockSpec((1, 2 * d), lambda b, j: (0, 0)),
                  pl.BlockSpec((1, LANES), lambda b, j: (0, 0))],
        out_specs=[pl.BlockSpec((1, tm, 3 * att_w), lambda b, j: (b, j, 0)),
                   pl.BlockSpec((1, tm, ssm_w), lambda b, j: (b, j, 0)),
                   pl.BlockSpec((1, tm, 2 * d), lambda b, j: (b, j, 0)),
                   pl.BlockSpec((1, tm, LANES), lambda b, j: (b, j, 0)),
                   pl.BlockSpec((1, tm, LANES), lambda b, j: (b, j, 0))],
        out_shape=[jax.ShapeDtypeStruct((bsz, seq, 3 * att_w), BF16),
                   jax.ShapeDtypeStruct((bsz, seq, ssm_w), F32),
                   jax.ShapeDtypeStruct((bsz, seq, 2 * d), BF16),
                   jax.ShapeDtypeStruct((bsz, seq, LANES), BF16),
                   jax.ShapeDtypeStruct((bsz, seq, LANES), BF16)],
        scratch_shapes=[pltpu.VMEM((8, LANES), F32)],
        compiler_params=_params(("arbitrary", "arbitrary")),
        name="inproj",
    )(x, ada_l, g, w_all, b_gate, b_forget_pad)


def _attn_kernel(q_ref, qa_ref, k_ref, ka_ref, v_ref, o_ref, *, tq):
    hp = pl.program_id(1)
    i = pl.program_id(2)
    half = tq // 2
    lane = lax.broadcasted_iota(jnp.int32, (tq, LANES), 1)
    first = lane < ATT_HEAD_DIM
    q2 = q_ref[0]
    qa = qa_ref[0]
    zero = jnp.zeros_like(q2)
    heads = []
    for hh in range(2):
        qsel = jnp.where(first, q2, zero) if hh == 0 else jnp.where(first, zero, q2)
        asel = jnp.where((lane & (ATT_HEADS - 1)) == 2 * hp + hh, qa, zero)
        heads.append(jnp.concatenate([qsel, asel], axis=1))

    def kv(start, size):
        kj = jnp.concatenate([k_ref[0, pl.ds(start, size), :], ka_ref[0, pl.ds(start, size), :]], axis=1)
        v2 = v_ref[0, pl.ds(start, size), :]
        one = jnp.ones_like(v2)
        lk = lax.broadcasted_iota(jnp.int32, v2.shape, 1) < ATT_HEAD_DIM
        return kj, (jnp.where(lk, v2, one), jnp.where(lk, one, v2))

    def update(qh, kj, vh, m, acc, mask):
        s = lax.dot_general(qh, kj, (((1,), (1,)), ((), ())), preferred_element_type=F32)
        if mask is not None:
            s = jnp.where(mask, s, -jnp.inf)
        m_new = jnp.maximum(m, jnp.max(s, axis=-1, keepdims=True))
        p = jnp.exp2(s - m_new).astype(BF16)
        acc = jnp.exp2(m - m_new) * acc + _dot(p, vh)
        return m_new, acc

    def body(j, carry):
        kj, vs = kv(pl.multiple_of(j * tq, tq), tq)
        out = []
        for hh in range(2):
            out.extend(update(heads[hh], kj, vs[hh], carry[2 * hh], carry[2 * hh + 1], None))
        return tuple(out)

    init = []
    for hh in range(2):
        init += [jnp.full((tq, 1), -jnp.inf, F32), jnp.zeros((tq, LANES), F32)]
    carry = lax.fori_loop(0, i, body, tuple(init))

    base = pl.multiple_of(i * tq, tq)
    r_top = lax.broadcasted_iota(jnp.int32, (half, half), 0)
    c_top = lax.broadcasted_iota(jnp.int32, (half, half), 1)
    r_bot = lax.broadcasted_iota(jnp.int32, (half, tq), 0)
    c_bot = lax.broadcasted_iota(jnp.int32, (half, tq), 1)
    k_top, v_top = kv(base, half)
    k_all, v_all = kv(base, tq)
    outs = []
    for hh in range(2):
        m, acc = carry[2 * hh], carry[2 * hh + 1]
        qh = heads[hh]
        _, acc_t = update(qh[:half], k_top, v_top[hh], m[:half], acc[:half], r_top >= c_top)
        _, acc_b = update(qh[half:], k_all, v_all[hh], m[half:], acc[half:], r_bot + half >= c_bot)
        acc = jnp.concatenate([acc_t, acc_b], axis=0)
        outs.append(acc / pltpu.roll(acc, ATT_HEAD_DIM, 1))
    o_ref[0] = jnp.where(first, outs[0], outs[1]).astype(BF16)


def _attention(qkv, qaug, kaug, *, tq):
    bsz, seq, w3 = qkv.shape
    att_w = w3 // 3
    npair = att_w // LANES
    kern = functools.partial(_attn_kernel, tq=tq)
    return pl.pallas_call(
        kern,
        grid=(bsz, npair, seq // tq),
        in_specs=[pl.BlockSpec((1, tq, LANES), lambda b, p, i: (b, i, p)),
                  pl.BlockSpec((1, tq, LANES), lambda b, p, i: (b, i, 0)),
                  pl.BlockSpec((1, seq, LANES), lambda b, p, i: (b, 0, npair + p)),
                  pl.BlockSpec((1, seq, LANES), lambda b, p, i: (b, 0, 0)),
                  pl.BlockSpec((1, seq, LANES), lambda b, p, i: (b, 0, 2 * npair + p))],
        out_specs=pl.BlockSpec((1, tq, LANES), lambda b, p, i: (b, i, p)),
        out_shape=jax.ShapeDtypeStruct((bsz, seq, att_w), BF16),
        compiler_params=_params(("arbitrary", "arbitrary", "arbitrary")),
        name="fox_attention",
    )(qkv, qaug, qkv, kaug, qkv)


def _split(x):
    hi = x.astype(BF16)
    return hi, (x - hi.astype(F32)).astype(BF16)


def _dot_nt3(a, b):
    dims = (((1,), (1,)), ((), ()))
    ah, al = _split(a)
    bh, bl = _split(b)
    nt = lambda u, v: lax.dot_general(u, v, dims, preferred_element_type=F32)
    return nt(ah, bh) + nt(ah, bl) + nt(al, bh)


def _dot_rep3(x, rep):
    p0 = x.astype(BF16)
    r0 = x - p0.astype(F32)
    p1 = r0.astype(BF16)
    p2 = (r0 - p1.astype(F32)).astype(BF16)
    return _dot(p0, rep) + _dot(p1, rep) + _dot(p2, rep)


def _s5_operator_kernel(lr_ref, li_ref, dt_ref, br_ref, bi_ref, cr_ref, ci_ref,
                        t_ref, b_ref, ct_ref, a1_ref, a2_ref, *, hc, p):
    gl = LANES // hc
    half = gl * p
    lr, li, dt = lr_ref[0], li_ref[0], jnp.exp(dt_ref[0])
    br, bi, cr, ci = br_ref[0], bi_ref[0], cr_ref[0], ci_ref[0]
    mag = jnp.exp(lr * dt)
    ang = li * dt
    a_re, a_im = mag * jnp.cos(ang), mag * jnp.sin(ang)
    n_re, n_im = a_re - 1.0, a_im
    den = lr * lr + li * li
    k_re = (n_re * lr + n_im * li) / den
    k_im = (n_im * lr - n_re * li) / den
    bb_re = k_re * br - k_im * bi
    bb_im = k_re * bi + k_im * br

    def power(t):
        m = jnp.exp(lr * dt * t)
        return m * jnp.cos(ang * t), m * jnp.sin(ang * t)

    pw = [power(float(t)) for t in range(SUB + 1)]

    rep = (lax.broadcasted_iota(jnp.int32, (p, half), 1) % p
           == lax.broadcasted_iota(jnp.int32, (p, half), 0)).astype(BF16)
    own = (lax.broadcasted_iota(jnp.int32, (LANES, half), 0) // hc
           == lax.broadcasted_iota(jnp.int32, (LANES, half), 1) // p)

    def expand(x):
        return jnp.where(own, _dot(x.astype(BF16), rep), 0.0).astype(BF16)

    same = (lax.broadcasted_iota(jnp.int32, (LANES, LANES), 0) // hc
            == lax.broadcasted_iota(jnp.int32, (LANES, LANES), 1) // hc)
    zero = jnp.zeros((LANES, LANES), BF16)
    kern = []
    for t in range(SUB):
        pr, pi = pw[t]
        wr, wi = cr * pr - ci * pi, cr * pi + ci * pr
        k = _dot_nt3(bb_re, wr) - _dot_nt3(bb_im, wi)
        kern.append(jnp.where(same, k, 0.0).astype(BF16))
    for j in range(SUB):
        for i in range(SUB):
            t_ref[0, j * LANES:(j + 1) * LANES, i * LANES:(i + 1) * LANES] = kern[i - j] if i >= j else zero
        pr, pi = pw[SUB - 1 - j]
        b_ref[0, j * LANES:(j + 1) * LANES, 0:half] = expand(pr * bb_re - pi * bb_im)
        b_ref[0, j * LANES:(j + 1) * LANES, half:2 * half] = expand(pr * bb_im + pi * bb_re)
        pr, pi = pw[j + 1]
        ct_ref[0, j * LANES:(j + 1) * LANES, 0:half] = expand(cr * pr - ci * pi)
        ct_ref[0, j * LANES:(j + 1) * LANES, half:2 * half] = expand(-(cr * pi + ci * pr))
    pr, pi = pw[SUB]
    along = lambda x: jnp.sum(jnp.where(own, _dot_rep3(x, rep), 0.0), axis=0, keepdims=True) * (1.0 / hc)
    ac_re, ac_im = along(pr), along(pi)
    a1_ref[0] = jnp.concatenate([ac_re, ac_re], axis=1)
    a2_ref[0] = jnp.concatenate([-ac_im, ac_im], axis=1)


def _s5_operators(lam_re, lam_im, log_dt, b_re, b_im, c_re, c_im):
    g, p = lam_re.shape
    hc = b_re.shape[-1]
    nt = g * hc // LANES
    ch = SUB * LANES
    ns = 2 * (LANES // hc) * p
    rows = lambda v: jnp.broadcast_to(v[:, None, :], (g, hc, v.shape[-1])).reshape(nt, LANES, v.shape[-1])
    compact = [rows(lam_re.astype(F32)), rows(lam_im.astype(F32)),
               rows(jnp.broadcast_to(log_dt.astype(F32)[:, None], (g, p))),
               b_re.astype(F32).transpose(0, 2, 1).reshape(nt, LANES, p),
               b_im.astype(F32).transpose(0, 2, 1).reshape(nt, LANES, p),
               c_re.astype(F32).reshape(nt, LANES, p), c_im.astype(F32).reshape(nt, LANES, p)]
    kern = functools.partial(_s5_operator_kernel, hc=hc, p=p)
    return pl.pallas_call(
        kern,
        grid=(nt,),
        in_specs=[pl.BlockSpec((1, LANES, p), lambda t: (t, 0, 0))] * 7,
        out_specs=[pl.BlockSpec((1, ch, ch), lambda t: (t, 0, 0)),
                   pl.BlockSpec((1, ch, ns), lambda t: (t, 0, 0)),
                   pl.BlockSpec((1, ch, ns), lambda t: (t, 0, 0)),
                   pl.BlockSpec((1, 1, ns), lambda t: (t, 0, 0)),
                   pl.BlockSpec((1, 1, ns), lambda t: (t, 0, 0))],
        out_shape=[jax.ShapeDtypeStruct((nt, ch, ch), BF16), jax.ShapeDtypeStruct((nt, ch, ns), BF16),
                   jax.ShapeDtypeStruct((nt, ch, ns), BF16), jax.ShapeDtypeStruct((nt, 1, ns), F32),
                   jax.ShapeDtypeStruct((nt, 1, ns), F32)],
        compiler_params=_params(("arbitrary",)),
        name="s5_operators",
    )(*compact)


def _s5_kernel(u_ref, t_ref, b_ref, ct_ref, a1_ref, a2_ref, d_ref, y_ref, uf_ref, s_ref, *, bb, nk):
    nc = s_ref.shape[0]
    for b in range(bb):
        for j in range(SUB):
            uf_ref[b * nk:(b + 1) * nk, j * LANES:(j + 1) * LANES] = (
                u_ref[b, pl.ds(j, nk, stride=SUB), :].astype(BF16))
    for b in range(bb):
        rows = slice(b * nk, (b + 1) * nk)
        s_loc = _dot(uf_ref[rows, :], b_ref[0])
        for c in range(nc):
            s_ref[c, rows, :] = s_loc[:, c * LANES:(c + 1) * LANES]
    a1 = a1_ref[0]
    a2 = a2_ref[0]

    def carry_step(k, state):
        rows = pl.ds(k, bb, stride=nk)
        loc = jnp.concatenate([s_ref[c, rows, :] for c in range(nc)], axis=1)
        for c in range(nc):
            s_ref[c, rows, :] = state[:, c * LANES:(c + 1) * LANES]
        return a1 * state + a2 * pltpu.roll(state, nc * LANES // 2, 1) + loc

    lax.fori_loop(0, nk, carry_step, jnp.zeros((bb, nc * LANES), F32), unroll=4)
    d = d_ref[0]
    for b in range(bb):
        rows = slice(b * nk, (b + 1) * nk)
        s_prev = jnp.concatenate([s_ref[c, rows, :] for c in range(nc)], axis=1).astype(BF16)
        y = _dot(uf_ref[rows, :], t_ref[0]) + lax.dot_general(
            s_prev, ct_ref[0], (((1,), (1,)), ((), ())), preferred_element_type=F32)
        for i in range(SUB):
            yi = y[:, i * LANES:(i + 1) * LANES] + d * u_ref[b, pl.ds(i, nk, stride=SUB), :]
            y_ref[b, pl.ds(i, nk, stride=SUB), :] = jax.nn.gelu(yi)


def _s5(u, weights, *, bb):
    bsz, seq, w = u.shape
    t_intra, b_state, c_state, a1, a2, dvec = weights
    nt = t_intra.shape[0]
    nk = seq // SUB
    ch = SUB * LANES
    ns = b_state.shape[-1]
    kern = functools.partial(_s5_kernel, bb=bb, nk=nk)
    return pl.pallas_call(
        kern,
        grid=(nt, bsz // bb),
        in_specs=[pl.BlockSpec((bb, seq, LANES), lambda t, i: (i, 0, t)),
                  pl.BlockSpec((1, ch, ch), lambda t, i: (t, 0, 0)),
                  pl.BlockSpec((1, ch, ns), lambda t, i: (t, 0, 0)),
                  pl.BlockSpec((1, ns, ch), lambda t, i: (t, 0, 0)),
                  pl.BlockSpec((1, 1, ns), lambda t, i: (t, 0, 0)),
                  pl.BlockSpec((1, 1, ns), lambda t, i: (t, 0, 0)),
                  pl.BlockSpec((1, 1, LANES), lambda t, i: (t, 0, 0))],
        out_specs=pl.BlockSpec((bb, seq, LANES), lambda t, i: (i, 0, t)),
        out_shape=jax.ShapeDtypeStruct((bsz, seq, w), F32),
        scratch_shapes=[pltpu.VMEM((bb * nk, ch), BF16), pltpu.VMEM((ns // LANES, bb * nk, LANES), F32)],
        compiler_params=_params(("arbitrary", "arbitrary")),
        name="s5_chunked",
    )(u, t_intra, b_state, c_state, a1, a2, dvec)


def _route(logits):
    lane = lax.broadcasted_iota(jnp.int32, logits.shape, 1)
    valid = lane < N_EXPERTS
    neg = jnp.float32(-jnp.inf)
    lg = jnp.where(valid, logits, neg)
    m1 = jnp.max(lg, axis=-1, keepdims=True)
    i1 = jnp.min(jnp.where(lg == m1, lane, LANES), axis=-1, keepdims=True)
    lg2 = jnp.where(lane == i1, neg, lg)
    m2 = jnp.max(lg2, axis=-1, keepdims=True)
    i2 = jnp.min(jnp.where(lg2 == m2, lane, LANES), axis=-1, keepdims=True)
    e2 = jnp.exp(m2 - m1)
    den = 1.0 + e2
    gates = jnp.where(lane == i1, 1.0 / den, 0.0) + jnp.where(lane == i2, e2 / den, 0.0)
    return gates, (lane == i1) | (lane == i2)


def _mix_kernel(att_ref, ys_ref, gate_ref, x_ref, ada_ref, wpa_ref, wglu_ref, bglu_ref, wps_ref,
                wout_ref, gffn_ref, wr_ref, br_ref, xo_ref, h2_ref, *route_refs, d, tm, group):
    ada = ada_ref[0]
    ya = _dot(att_ref[0], wpa_ref[...])
    ys = ys_ref[0]
    glu = ys * jax.nn.sigmoid(_dot(ys.astype(BF16), wglu_ref[...]) + bglu_ref[...])
    yp = _dot(glu.astype(BF16), wps_ref[...])
    mixed = gate_ref[0, :, 0:d].astype(F32) * ya + gate_ref[0, :, d:2 * d].astype(F32) * yp
    x_new = x_ref[0] + ada[2:3] * _dot(mixed.astype(BF16), wout_ref[...])
    xo_ref[0] = x_new
    h2 = _modulated_norm(x_new, gffn_ref[...], ada[3:4], ada[4:5])
    h2b = h2.astype(BF16)
    h2_ref[0] = h2b
    if not route_refs:
        return
    rkr_ref, gwr_ref, cnt_ref = route_refs
    h2l = (h2 - h2b.astype(F32)).astype(BF16)
    both = _dot(h2b, wr_ref[...])
    logits = (both[:, 0:LANES] + both[:, LANES:2 * LANES] + _dot(h2l, wr_ref[:, 0:LANES])) + br_ref[...]
    gates, mask = _route(logits)

    @pl.when(pl.program_id(1) % group == 0)
    def _():
        cnt_ref[...] = jnp.zeros_like(cnt_ref)

    row = lax.broadcasted_iota(jnp.int32, (tm, tm), 0)
    col = lax.broadcasted_iota(jnp.int32, (tm, tm), 1)
    maskf = mask.astype(F32)
    before = _dot((row > col).astype(BF16), maskf.astype(BF16)) + cnt_ref[0:1, :]
    rank = jnp.where(mask, before, -1.0)
    cnt_ref[...] = jnp.broadcast_to(before[tm - 1:tm, :] + maskf[tm - 1:tm, :], cnt_ref.shape)
    rkr_ref[...] = jnp.transpose(rank)[0:N_EXPERTS, :]
    gwr_ref[...] = jnp.transpose(gates)[0:N_EXPERTS, :]


def _mix(att, ys, gates, x, ada_l, wpa, wglu, bglu, wps, wout, gffn, wr_split, b_router, *, tm, group):
    bsz, seq, d = x.shape
    aw = att.shape[-1]
    sw = ys.shape[-1]
    nj = seq // tm
    kern = functools.partial(_mix_kernel, d=d, tm=tm, group=group)
    const = lambda b, j: (0, 0)
    tile = lambda b, j: (b, j, 0)
    out_specs = [pl.BlockSpec((1, tm, d), tile), pl.BlockSpec((1, tm, d), tile)]
    out_shape = [jax.ShapeDtypeStruct((bsz, seq, d), F32), jax.ShapeDtypeStruct((bsz, seq, d), BF16)]
    scratch = []
    if group:
        flat = lambda b, j: (0, b * nj + j)
        out_specs += [pl.BlockSpec((N_EXPERTS, tm), flat), pl.BlockSpec((N_EXPERTS, tm), flat)]
        out_shape += [jax.ShapeDtypeStruct((N_EXPERTS, bsz * seq), F32),
                      jax.ShapeDtypeStruct((N_EXPERTS, bsz * seq), F32)]
        scratch = [pltpu.VMEM((SUB, LANES), F32)]
    return pl.pallas_call(
        kern,
        grid=(bsz, nj),
        in_specs=[pl.BlockSpec((1, tm, aw), tile),
                  pl.BlockSpec((1, tm, sw), tile),
                  pl.BlockSpec((1, tm, 2 * d), tile),
                  pl.BlockSpec((1, tm, d), tile),
                  pl.BlockSpec((1, 6, d), lambda b, j: (b, 0, 0)),
                  pl.BlockSpec((aw, d), const),
                  pl.BlockSpec((sw, sw), const),
                  pl.BlockSpec((1, sw), const),
                  pl.BlockSpec((sw, d), const),
                  pl.BlockSpec((d, d), const),
                  pl.BlockSpec((1, d), const),
                  pl.BlockSpec((d, 2 * LANES), const),
                  pl.BlockSpec((1, LANES), const)],
        out_specs=out_specs,
        out_shape=out_shape,
        scratch_shapes=scratch,
        compiler_params=_params(("arbitrary", "arbitrary")),
        name="mix",
    )(att, ys, gates, x, ada_l, wpa, wglu, bglu, wps, wout, gffn, wr_split, b_router)


def _residual_out(x, ada, acc, gfin_ref, final_norm):
    y = x + ada[5:6] * acc
    if final_norm:
        y = y * lax.rsqrt(jnp.mean(y * y, axis=-1, keepdims=True) + EPS) * gfin_ref[...]
    return y


def _ffn_kernel(h_ref, x_ref, ada_ref, w1_ref, w3_ref, w2_ref, gfin_ref, o_ref, acc_ref, *, nf, final_norm):
    f = pl.program_id(1)

    @pl.when(f == 0)
    def _():
        acc_ref[...] = jnp.zeros_like(acc_ref)

    h = h_ref[...]
    act = jax.nn.silu(_dot(h, w1_ref[...])) * _dot(h, w3_ref[...])
    acc_ref[...] += _dot(act.astype(BF16), w2_ref[...])

    @pl.when(f == nf - 1)
    def _():
        o_ref[...] = _residual_out(x_ref[...], ada_ref[0], acc_ref[...], gfin_ref, final_norm)


def _ffn(h2, x, ada_l, w1, w3, w2, g_final, *, tm, cf, seq, final_norm):
    t, d = h2.shape
    dff = w1.shape[-1]
    nf = dff // cf
    tiles_per_seq = seq // tm
    kern = functools.partial(_ffn_kernel, nf=nf, final_norm=final_norm)
    return pl.pallas_call(
        kern,
        grid=(t // tm, nf),
        in_specs=[pl.BlockSpec((tm, d), lambda i, f: (i, 0)),
                  pl.BlockSpec((tm, d), lambda i, f: (i, 0)),
                  pl.BlockSpec((1, 6, d), lambda i, f: (i // tiles_per_seq, 0, 0)),
                  pl.BlockSpec((d, cf), lambda i, f: (0, f)),
                  pl.BlockSpec((d, cf), lambda i, f: (0, f)),
                  pl.BlockSpec((cf, d), lambda i, f: (f, 0)),
                  pl.BlockSpec((1, d), lambda i, f: (0, 0))],
        out_specs=pl.BlockSpec((tm, d), lambda i, f: (i, 0)),
        out_shape=jax.ShapeDtypeStruct((t, d), F32),
        scratch_shapes=[pltpu.VMEM((tm, d), F32)],
        compiler_params=_params(("arbitrary", "arbitrary")),
        name="ffn",
    )(h2, x, ada_l, w1, w3, w2, g_final)


def _moe_kernel(h_ref, rkr_ref, gwr_ref, x_ref, ada_ref, w1_ref, w3_ref, w2_ref, gfin_ref, o_ref,
                *, n_exp, rows, final_norm):
    e = pl.program_id(1)
    tf = h_ref.shape[0]

    @pl.when(e == 0)
    def _():
        o_ref[...] = jnp.zeros_like(o_ref)

    rk_row = rkr_ref[pl.ds(e, 1), :]
    gw_row = gwr_ref[pl.ds(e, 1), :]
    count = jnp.max(rk_row).astype(jnp.int32) + 1

    r_sub = lax.broadcasted_iota(jnp.int32, (rows, tf), 0).astype(F32)

    def chunk(c, carry):
        pick = rk_row == r_sub + (c * rows).astype(F32)
        pickb = pick.astype(BF16)
        xc = _dot(pickb, h_ref[...]).astype(BF16)
        gate = jnp.sum(jnp.where(pick, gw_row, 0.0), axis=-1, keepdims=True)
        act = jax.nn.silu(_dot(xc, w1_ref[0])) * _dot(xc, w3_ref[0]) * gate
        yc = _dot(act.astype(BF16), w2_ref[0]).astype(BF16)
        o_ref[...] += lax.dot_general(pickb, yc, (((0,), (0,)), ((), ())), preferred_element_type=F32)
        return carry

    lax.fori_loop(0, (count + rows - 1) // rows, chunk, 0)

    @pl.when(e == n_exp - 1)
    def _():
        o_ref[...] = _residual_out(x_ref[...], ada_ref[0], o_ref[...], gfin_ref, final_norm)


def _moe(h2, rk_row, gw_row, x, ada_l, w1, w3, w2, g_final, *, tf, rows, seq, final_norm):
    t, d = h2.shape
    n_exp, _, dff = w1.shape
    tiles_per_seq = seq // tf
    kern = functools.partial(_moe_kernel, n_exp=n_exp, rows=rows, final_norm=final_norm)
    once = pl.Buffered(1)
    return pl.pallas_call(
        kern,
        grid=(t // tf, n_exp),
        in_specs=[pl.BlockSpec((tf, d), lambda i, e: (i, 0), pipeline_mode=once),
                  pl.BlockSpec((N_EXPERTS, tf), lambda i, e: (0, i), pipeline_mode=once),
                  pl.BlockSpec((N_EXPERTS, tf), lambda i, e: (0, i), pipeline_mode=once),
                  pl.BlockSpec((tf, d), lambda i, e: (i, 0), pipeline_mode=once),
                  pl.BlockSpec((1, 6, d), lambda i, e: (i // tiles_per_seq, 0, 0)),
                  pl.BlockSpec((1, d, dff), lambda i, e: (e, 0, 0)),
                  pl.BlockSpec((1, d, dff), lambda i, e: (e, 0, 0)),
                  pl.BlockSpec((1, dff, d), lambda i, e: (e, 0, 0)),
                  pl.BlockSpec((1, d), lambda i, e: (0, 0))],
        out_specs=pl.BlockSpec((tf, d), lambda i, e: (i, 0)),
        out_shape=jax.ShapeDtypeStruct((t, d), F32),
        compiler_params=_params(("arbitrary", "arbitrary")),
        name="moe",
    )(h2, rk_row, gw_row, x, ada_l, w1, w3, w2, g_final)


def _tiles(seq, bsz):
    ti = min(256, seq)
    tm = min(1024, seq)
    tq = min(1024, seq // 2)
    tf = min(1024, seq)
    return ti, tm, tq, min(4, bsz), tf, 144


def kernel(x, c, w_ada, b_ada, g_mix, w_in, b_forget, b_gate, lam_re, lam_im, log_dt, b_re, b_im,
           c_re, c_im, d_skip, w_glu, b_glu, w_proj_att, w_proj_ssm, w_out, g_ffn, w1_dense, w3_dense,
           w2_dense, w_router, b_router, w1_moe, w3_moe, w2_moe, g_final):
    bsz, seq, d = x.shape
    depth = w_ada.shape[0]
    att_w = ATT_HEADS * ATT_HEAD_DIM
    ssm_w = w_glu.shape[-1]
    dff = w1_dense.shape[-1]
    ti, tm, tq, bb, tf, rows = _tiles(seq, bsz)
    cf = dff // 2

    ada = _ada(c, w_ada, b_ada).reshape(depth, bsz, 6, d)
    gfin = g_final.reshape(1, d)

    for l in range(depth):
        o1, o2, o3 = 3 * att_w, 3 * att_w + ATT_HEADS, 3 * att_w + ATT_HEADS + ssm_w
        w = w_in[l]
        w_fg = jnp.pad(w[:, o1:o2], ((0, 0), (0, LANES - ATT_HEADS)))
        w_all = jnp.concatenate([w[:, :o1], w[:, o2:o3], w[:, o3:], w_fg], axis=1).astype(BF16)
        bf_pad = jnp.pad(b_forget[l], (0, LANES - ATT_HEADS)).reshape(1, LANES)

        qkv, u, gates, qaug, kaug = _inproj(x, ada[l], g_mix[l].reshape(1, d), w_all,
                                      b_gate[l].reshape(1, 2 * d), bf_pad,
                                      tm=ti, att_w=att_w, ssm_w=ssm_w)

        att = _attention(qkv, qaug, kaug, tq=tq)

        s5w = _s5_operators(lam_re[l], lam_im[l], log_dt[l], b_re[l], b_im[l], c_re[l], c_im[l])
        ys = _s5(u, (*s5w, d_skip[l].astype(F32).reshape(-1, 1, LANES)), bb=bb)

        moe = l % 2 == 1
        m = l // 2
        if moe:
            wr = jnp.pad(w_router[m], ((0, 0), (0, LANES - N_EXPERTS)))
            br = jnp.pad(b_router[m], (0, LANES - N_EXPERTS)).reshape(1, LANES)
        else:
            wr = jnp.zeros((d, LANES), F32)
            br = jnp.zeros((1, LANES), F32)
        wr_hi = wr.astype(BF16)
        wr_split = jnp.concatenate([wr_hi, (wr - wr_hi.astype(F32)).astype(BF16)], axis=1)
        mixed = _mix(att, ys, gates, x, ada[l],
                     w_proj_att[l].astype(BF16), w_glu[l].astype(BF16),
                     b_glu[l].reshape(1, ssm_w), w_proj_ssm[l].astype(BF16),
                     w_out[l].astype(BF16), g_ffn[l].reshape(1, d), wr_split, br,
                     tm=tm, group=(tf // tm if moe else 0))
        x, h2 = mixed[0].reshape(bsz * seq, d), mixed[1].reshape(bsz * seq, d)
        last = l == depth - 1
        if moe:
            x = _moe(h2, mixed[2], mixed[3], x, ada[l],
                     w1_moe[m].astype(BF16), w3_moe[m].astype(BF16), w2_moe[m].astype(BF16), gfin,
                     tf=tf, rows=rows, seq=seq, final_norm=last)
        else:
            x = _ffn(h2, x, ada[l], w1_dense[m].astype(BF16), w3_dense[m].astype(BF16),
                     w2_dense[m].astype(BF16), gfin, tm=tf, cf=cf, seq=seq, final_norm=last)
        x = x.reshape(bsz, seq, d)
    return x
```

```python
import functools
import math

import jax
import jax.numpy as jnp
from jax import lax
from jax.experimental import pallas as pl
from jax.experimental.pallas import tpu as pltpu

F32 = jnp.float32
BF16 = jnp.bfloat16
EPS = 1e-6

ATT_HEADS = 8
ATT_HEAD_DIM = 64
SSM_GROUP = 16
SSM_STATE = 64
N_EXPERTS = 8
LANES = 128
SUB = 8
LOG2E = math.log2(math.e)
F_OFF = 32
VMEM_LIMIT = 56 * 1024 * 1024


def _params(sem):
    return pltpu.CompilerParams(dimension_semantics=sem, vmem_limit_bytes=VMEM_LIMIT)


def _dot(a, b):
    return jnp.dot(a, b, preferred_element_type=F32)


def _ada_kernel(c_ref, w_ref, b_ref, o_ref):
    cond = jax.nn.silu(c_ref[...])
    o_ref[0] = jnp.dot(cond, w_ref[0], preferred_element_type=F32,
                       precision=lax.Precision.HIGHEST) + b_ref[0]


def _ada(c, w_ada, b_ada, tn=1536):
    depth, d, n = w_ada.shape
    bsz = c.shape[0]
    return pl.pallas_call(
        _ada_kernel,
        grid=(depth, n // tn),
        in_specs=[pl.BlockSpec((bsz, d), lambda l, j: (0, 0)),
                  pl.BlockSpec((1, d, tn), lambda l, j: (l, 0, j)),
                  pl.BlockSpec((1, 1, tn), lambda l, j: (l, 0, j))],
        out_specs=pl.BlockSpec((1, bsz, tn), lambda l, j: (l, 0, j)),
        out_shape=jax.ShapeDtypeStruct((depth, bsz, n), F32),
        compiler_params=_params(("arbitrary", "arbitrary")),
        name="ada",
    )(c, w_ada, b_ada.reshape(depth, 1, n))


def _modulated_norm(x, g, shift, scale):
    y = x * lax.rsqrt(jnp.mean(x * x, axis=-1, keepdims=True) + EPS) * g
    return y * (1.0 + scale) + shift


def _inproj_kernel(x_ref, ada_ref, g_ref, w_ref, bg_ref, bf_ref,
                   qkv_ref, u_ref, gate_ref, qa_ref, ka_ref, carry_ref, *, tm, att_w, ssm_w, d):
    j = pl.program_id(1)
    ada = ada_ref[0]
    h = _modulated_norm(x_ref[0], g_ref[...], ada[0:1], ada[1:2]).astype(BF16)

    cw = att_w
    qkv_ref[0, :, 0:cw] = (_dot(h, w_ref[:, 0:cw]) * (ATT_HEAD_DIM ** -0.5 * LOG2E)).astype(BF16)
    for c0 in range(cw, 3 * att_w, cw):
        qkv_ref[0, :, c0:c0 + cw] = _dot(h, w_ref[:, c0:c0 + cw]).astype(BF16)
    o_u = 3 * att_w
    u_ref[0] = _dot(h, w_ref[:, o_u:o_u + ssm_w])
    o_g = o_u + ssm_w
    for c0 in range(0, 2 * d, cw):
        z = _dot(h, w_ref[:, o_g + c0:o_g + c0 + cw]) + bg_ref[:, c0:c0 + cw]
        gate_ref[0, :, c0:c0 + cw] = jax.nn.sigmoid(z).astype(BF16)

    o_f = o_g + 2 * d
    logf = jax.nn.log_sigmoid(_dot(h, w_ref[:, o_f:o_f + LANES]) + bf_ref[...])
    row = lax.broadcasted_iota(jnp.int32, (tm, tm), 0)
    col = lax.broadcasted_iota(jnp.int32, (tm, tm), 1)
    tri = (row >= col).astype(BF16)
    hi = logf.astype(BF16)
    r1 = logf - hi.astype(F32)
    mid = r1.astype(BF16)
    lo = (r1 - mid.astype(F32)).astype(BF16)
    cs = _dot(tri, hi) + _dot(tri, mid) + _dot(tri, lo)

    @pl.when(j == 0)
    def _():
        carry_ref[...] = jnp.zeros_like(carry_ref)

    fcum = cs + carry_ref[0:1, :]
    carry_ref[...] = jnp.broadcast_to(fcum[tm - 1:tm, :], carry_ref.shape)

    lane = lax.broadcasted_iota(jnp.int32, (tm, LANES), 1)
    fl = jnp.where(lane < ATT_HEADS, fcum * LOG2E, 0.0)
    p0 = fl.astype(BF16).astype(F32)
    r0 = fl - p0
    p1 = r0.astype(BF16).astype(F32)
    p2 = (r0 - p1).astype(BF16).astype(F32)
    comb = p0 + pltpu.roll(p1, ATT_HEADS, 1) + pltpu.roll(p2, 2 * ATT_HEADS, 1)
    sel_lo = (lane < 3 * ATT_HEADS).astype(F32)
    sel_hi = ((lane >= F_OFF) & (lane < F_OFF + 3 * ATT_HEADS)).astype(F32)
    qa_ref[0] = (pltpu.roll(comb, F_OFF, 1) + sel_lo).astype(BF16)
    ka_ref[0] = (sel_hi - comb).astype(BF16)


def _inproj(x, ada_l, g, w_all, b_gate, b_forget_pad, *, tm, att_w, ssm_w):
    bsz, seq, d = x.shape
    n_all = w_all.shape[1]
    kern = functools.partial(_inproj_kernel, tm=tm, att_w=att_w, ssm_w=ssm_w, d=d)
    return pl.pallas_call(
        kern,
        grid=(bsz, seq // tm),
        in_specs=[pl.BlockSpec((1, tm, d), lambda b, j: (b, j, 0)),
                  pl.BlockSpec((1, 6, d), lambda b, j: (b, 0, 0)),
                  pl.BlockSpec((1, d), lambda b, j: (0, 0)),
                  pl.BlockSpec((d, n_all), lambda b, j: (0, 0)),
                  pl.BlockSpec((1, 2 * d), lambda b, j: (0, 0)),
                  pl.BlockSpec((1, LANES), lambda b, j: (0, 0))],
        out_specs=[pl.BlockSpec((1, tm, 3 * att_w), lambda b, j: (b, j, 0)),
                   pl.BlockSpec((1, tm, ssm_w), lambda b, j: (b, j, 0)),
                   pl.BlockSpec((1, tm, 2 * d), lambda b, j: (b, j, 0)),
                   pl.BlockSpec((1, tm, LANES), lambda b, j: (b, j, 0)),
                   pl.BlockSpec((1, tm, LANES), lambda b, j: (b, j, 0))],
        out_shape=[jax.ShapeDtypeStruct((bsz, seq, 3 * att_w), BF16),
                   jax.ShapeDtypeStruct((bsz, seq, ssm_w), F32),
                   jax.ShapeDtypeStruct((bsz, seq, 2 * d), BF16),
                   jax.ShapeDtypeStruct((bsz, seq, LANES), BF16),
                   jax.ShapeDtypeStruct((bsz, seq, LANES), BF16)],
        scratch_shapes=[pltpu.VMEM((8, LANES), F32)],
        compiler_params=_params(("arbitrary", "arbitrary")),
        name="inproj",
    )(x, ada_l, g, w_all, b_gate, b_forget_pad)


def _attn_kernel(q_ref, qa_ref, k_ref, ka_ref, v_ref, o_ref, *, tq):
    hp = pl.program_id(1)
    i = pl.program_id(2)
    half = tq // 2
    lane = lax.broadcasted_iota(jnp.int32, (tq, LANES), 1)
    first = lane < ATT_HEAD_DIM
    q2 = q_ref[0]
    qa = qa_ref[0]
    zero = jnp.zeros_like(q2)
    heads = []
    for hh in range(2):
        qsel = jnp.where(first, q2, zero) if hh == 0 else jnp.where(first, zero, q2)
        asel = jnp.where((lane & (ATT_HEADS - 1)) == 2 * hp + hh, qa, zero)
        heads.append(jnp.concatenate([qsel, asel], axis=1))

    def kv(start, size):
        kj = jnp.concatenate([k_ref[0, pl.ds(start, size), :], ka_ref[0, pl.ds(start, size), :]], axis=1)
        v2 = v_ref[0, pl.ds(start, size), :]
        one = jnp.ones_like(v2)
        lk = lax.broadcasted_iota(jnp.int32, v2.shape, 1) < ATT_HEAD_DIM
        return kj, (jnp.where(lk, v2, one), jnp.where(lk, one, v2))

    def update(qh, kj, vh, m, acc, mask):
        s = lax.dot_general(qh, kj, (((1,), (1,)), ((), ())), preferred_element_type=F32)
        if mask is not None:
            s = jnp.where(mask, s, -jnp.inf)
        m_new = jnp.maximum(m, jnp.max(s, axis=-1, keepdims=True))
        p = jnp.exp2(s - m_new).astype(BF16)
        acc = jnp.exp2(m - m_new) * acc + _dot(p, vh)
        return m_new, acc

    def body(j, carry):
        kj, vs = kv(pl.multiple_of(j * tq, tq), tq)
        out = []
        for hh in range(2):
            out.extend(update(heads[hh], kj, vs[hh], carry[2 * hh], carry[2 * hh + 1], None))
        return tuple(out)

    init = []
    for hh in range(2):
        init += [jnp.full((tq, 1), -jnp.inf, F32), jnp.zeros((tq, LANES), F32)]
    carry = lax.fori_loop(0, i, body, tuple(init))

    base = pl.multiple_of(i * tq, tq)
    r_top = lax.broadcasted_iota(jnp.int32, (half, half), 0)
    c_top = lax.broadcasted_iota(jnp.int32, (half, half), 1)
    r_bot = lax.broadcasted_iota(jnp.int32, (half, tq), 0)
    c_bot = lax.broadcasted_iota(jnp.int32, (half, tq), 1)
    k_top, v_top = kv(base, half)
    k_all, v_all = kv(base, tq)
    outs = []
    for hh in range(2):
        m, acc = carry[2 * hh], carry[2 * hh + 1]
        qh = heads[hh]
        _, acc_t = update(qh[:half], k_top, v_top[hh], m[:half], acc[:half], r_top >= c_top)
        _, acc_b = update(qh[half:], k_all, v_all[hh], m[half:], acc[half:], r_bot + half >= c_bot)
        acc = jnp.concatenate([acc_t, acc_b], axis=0)
        outs.append(acc / pltpu.roll(acc, ATT_HEAD_DIM, 1))
    o_ref[0] = jnp.where(first, outs[0], outs[1]).astype(BF16)


def _attention(qkv, qaug, kaug, *, tq):
    bsz, seq, w3 = qkv.shape
    att_w = w3 // 3
    npair = att_w // LANES
    kern = functools.partial(_attn_kernel, tq=tq)
    return pl.pallas_call(
        kern,
        grid=(bsz, npair, seq // tq),
        in_specs=[pl.BlockSpec((1, tq, LANES), lambda b, p, i: (b, i, p)),
                  pl.BlockSpec((1, tq, LANES), lambda b, p, i: (b, i, 0)),
                  pl.BlockSpec((1, seq, LANES), lambda b, p, i: (b, 0, npair + p)),
                  pl.BlockSpec((1, seq, LANES), lambda b, p, i: (b, 0, 0)),
                  pl.BlockSpec((1, seq, LANES), lambda b, p, i: (b, 0, 2 * npair + p))],
        out_specs=pl.BlockSpec((1, tq, LANES), lambda b, p, i: (b, i, p)),
        out_shape=jax.ShapeDtypeStruct((bsz, seq, att_w), BF16),
        compiler_params=_params(("arbitrary", "arbitrary", "arbitrary")),
        name="fox_attention",
    )(qkv, qaug, qkv, kaug, qkv)


def _split(x):
    hi = x.astype(BF16)
    return hi, (x - hi.astype(F32)).astype(BF16)


def _dot_nt3(a, b):
    dims = (((1,), (1,)), ((), ()))
    ah, al = _split(a)
    bh, bl = _split(b)
    nt = lambda u, v: lax.dot_general(u, v, dims, preferred_element_type=F32)
    return nt(ah, bh) + nt(ah, bl) + nt(al, bh)


def _dot_rep3(x, rep):
    p0 = x.astype(BF16)
    r0 = x - p0.astype(F32)
    p1 = r0.astype(BF16)
    p2 = (r0 - p1.astype(F32)).astype(BF16)
    return _dot(p0, rep) + _dot(p1, rep) + _dot(p2, rep)


def _s5_operator_kernel(lr_ref, li_ref, dt_ref, br_ref, bi_ref, cr_ref, ci_ref,
                        t_ref, b_ref, ct_ref, a1_ref, a2_ref, *, hc, p):
    gl = LANES // hc
    half = gl * p
    lr, li, dt = lr_ref[0], li_ref[0], jnp.exp(dt_ref[0])
    br, bi, cr, ci = br_ref[0], bi_ref[0], cr_ref[0], ci_ref[0]
    mag = jnp.exp(lr * dt)
    ang = li * dt
    a_re, a_im = mag * jnp.cos(ang), mag * jnp.sin(ang)
    n_re, n_im = a_re - 1.0, a_im
    den = lr * lr + li * li
    k_re = (n_re * lr + n_im * li) / den
    k_im = (n_im * lr - n_re * li) / den
    bb_re = k_re * br - k_im * bi
    bb_im = k_re * bi + k_im * br

    def power(t):
        m = jnp.exp(lr * dt * t)
        return m * jnp.cos(ang * t), m * jnp.sin(ang * t)

    pw = [power(float(t)) for t in range(SUB + 1)]

    rep = (lax.broadcasted_iota(jnp.int32, (p, half), 1) % p
           == lax.broadcasted_iota(jnp.int32, (p, half), 0)).astype(BF16)
    own = (lax.broadcasted_iota(jnp.int32, (LANES, half), 0) // hc
           == lax.broadcasted_iota(jnp.int32, (LANES, half), 1) // p)

    def expand(x):
        return jnp.where(own, _dot(x.astype(BF16), rep), 0.0).astype(BF16)

    same = (lax.broadcasted_iota(jnp.int32, (LANES, LANES), 0) // hc
            == lax.broadcasted_iota(jnp.int32, (LANES, LANES), 1) // hc)
    zero = jnp.zeros((LANES, LANES), BF16)
    kern = []
    for t in range(SUB):
        pr, pi = pw[t]
        wr, wi = cr * pr - ci * pi, cr * pi + ci * pr
        k = _dot_nt3(bb_re, wr) - _dot_nt3(bb_im, wi)
        kern.append(jnp.where(same, k, 0.0).astype(BF16))
    for j in range(SUB):
        for i in range(SUB):
            t_ref[0, j * LANES:(j + 1) * LANES, i * LANES:(i + 1) * LANES] = kern[i - j] if i >= j else zero
        pr, pi = pw[SUB - 1 - j]
        b_ref[0, j * LANES:(j + 1) * LANES, 0:half] = expand(pr * bb_re - pi * bb_im)
        b_ref[0, j * LANES:(j + 1) * LANES, half:2 * half] = expand(pr * bb_im + pi * bb_re)
        pr, pi = pw[j + 1]
        ct_ref[0, j * LANES:(j + 1) * LANES, 0:half] = expand(cr * pr - ci * pi)
        ct_ref[0, j * LANES:(j + 1) * LANES, half:2 * half] = expand(-(cr * pi + ci * pr))
    pr, pi = pw[SUB]
    along = lambda x: jnp.sum(jnp.where(own, _dot_rep3(x, rep), 0.0), axis=0, keepdims=True) * (1.0 / hc)
    ac_re, ac_im = along(pr), along(pi)
    a1_ref[0] = jnp.concatenate([ac_re, ac_re], axis=1)
    a2_ref[0] = jnp.concatenate([-ac_im, ac_im], axis=1)


def _s5_operators(lam_re, lam_im, log_dt, b_re, b_im, c_re, c_im):
    g, p = lam_re.shape
    hc = b_re.shape[-1]
    nt = g * hc // LANES
    ch = SUB * LANES
    ns = 2 * (LANES // hc) * p
    rows = lambda v: jnp.broadcast_to(v[:, None, :], (g, hc, v.shape[-1])).reshape(nt, LANES, v.shape[-1])
    compact = [rows(lam_re.astype(F32)), rows(lam_im.astype(F32)),
               rows(jnp.broadcast_to(log_dt.astype(F32)[:, None], (g, p))),
               b_re.astype(F32).transpose(0, 2, 1).reshape(nt, LANES, p),
               b_im.astype(F32).transpose(0, 2, 1).reshape(nt, LANES, p),
               c_re.astype(F32).reshape(nt, LANES, p), c_im.astype(F32).reshape(nt, LANES, p)]
    kern = functools.partial(_s5_operator_kernel, hc=hc, p=p)
    return pl.pallas_call(
        kern,
        grid=(nt,),
        in_specs=[pl.BlockSpec((1, LANES, p), lambda t: (t, 0, 0))] * 7,
        out_specs=[pl.BlockSpec((1, ch, ch), lambda t: (t, 0, 0)),
                   pl.BlockSpec((1, ch, ns), lambda t: (t, 0, 0)),
                   pl.BlockSpec((1, ch, ns), lambda t: (t, 0, 0)),
                   pl.BlockSpec((1, 1, ns), lambda t: (t, 0, 0)),
                   pl.BlockSpec((1, 1, ns), lambda t: (t, 0, 0))],
        out_shape=[jax.ShapeDtypeStruct((nt, ch, ch), BF16), jax.ShapeDtypeStruct((nt, ch, ns), BF16),
                   jax.ShapeDtypeStruct((nt, ch, ns), BF16), jax.ShapeDtypeStruct((nt, 1, ns), F32),
                   jax.ShapeDtypeStruct((nt, 1, ns), F32)],
        compiler_params=_params(("arbitrary",)),
        name="s5_operators",
    )(*compact)


def _s5_kernel(u_ref, t_ref, b_ref, ct_ref, a1_ref, a2_ref, d_ref, y_ref, uf_ref, s_ref, *, bb, nk):
    nc = s_ref.shape[0]
    for b in range(bb):
        for j in range(SUB):
            uf_ref[b * nk:(b + 1) * nk, j * LANES:(j + 1) * LANES] = (
                u_ref[b, pl.ds(j, nk, stride=SUB), :].astype(BF16))
    pitch = s_ref.shape[1] // bb
    for b in range(bb):
        s_loc = _dot(uf_ref[b * nk:(b + 1) * nk, :], b_ref[0])
        for c in range(nc):
            s_ref[c, b * pitch:b * pitch + nk, :] = s_loc[:, c * LANES:(c + 1) * LANES]
    a1 = a1_ref[0]
    a2 = a2_ref[0]

    def carry_step(k, state):
        rows = pl.ds(k, bb, stride=pitch)
        loc = jnp.concatenate([s_ref[c, rows, :] for c in range(nc)], axis=1)
        for c in range(nc):
            s_ref[c, rows, :] = state[:, c * LANES:(c + 1) * LANES]
        return a1 * state + a2 * pltpu.roll(state, nc * LANES // 2, 1) + loc

    lax.fori_loop(0, nk, carry_step, jnp.zeros((bb, nc * LANES), F32), unroll=4)
    d = d_ref[0]
    for b in range(bb):
        rows = slice(b * nk, (b + 1) * nk)
        s_prev = jnp.concatenate([s_ref[c, b * pitch:b * pitch + nk, :] for c in range(nc)],
                                 axis=1).astype(BF16)
        y = _dot(uf_ref[rows, :], t_ref[0]) + lax.dot_general(
            s_prev, ct_ref[0], (((1,), (1,)), ((), ())), preferred_element_type=F32)
        for i in range(SUB):
            yi = y[:, i * LANES:(i + 1) * LANES] + d * u_ref[b, pl.ds(i, nk, stride=SUB), :]
            y_ref[b, pl.ds(i, nk, stride=SUB), :] = jax.nn.gelu(yi)


def _s5(u, weights, *, bb):
    bsz, seq, w = u.shape
    t_intra, b_state, c_state, a1, a2, dvec = weights
    nt = t_intra.shape[0]
    nk = seq // SUB
    ch = SUB * LANES
    ns = b_state.shape[-1]
    kern = functools.partial(_s5_kernel, bb=bb, nk=nk)
    return pl.pallas_call(
        kern,
        grid=(nt, bsz // bb),
        in_specs=[pl.BlockSpec((bb, seq, LANES), lambda t, i: (i, 0, t)),
                  pl.BlockSpec((1, ch, ch), lambda t, i: (t, 0, 0)),
                  pl.BlockSpec((1, ch, ns), lambda t, i: (t, 0, 0)),
                  pl.BlockSpec((1, ns, ch), lambda t, i: (t, 0, 0)),
                  pl.BlockSpec((1, 1, ns), lambda t, i: (t, 0, 0)),
                  pl.BlockSpec((1, 1, ns), lambda t, i: (t, 0, 0)),
                  pl.BlockSpec((1, 1, LANES), lambda t, i: (t, 0, 0))],
        out_specs=pl.BlockSpec((bb, seq, LANES), lambda t, i: (i, 0, t)),
        out_shape=jax.ShapeDtypeStruct((bsz, seq, w), F32),
        scratch_shapes=[pltpu.VMEM((bb * nk, ch), BF16),
                        pltpu.VMEM((ns // LANES, bb * (nk + SUB * (1 + nk // SUB % 2)), LANES), F32)],
        compiler_params=_params(("arbitrary", "arbitrary")),
        name="s5_chunked",
    )(u, t_intra, b_state, c_state, a1, a2, dvec)


def _route(logits):
    lane = lax.broadcasted_iota(jnp.int32, logits.shape, 1)
    valid = lane < N_EXPERTS
    neg = jnp.float32(-jnp.inf)
    lg = jnp.where(valid, logits, neg)
    m1 = jnp.max(lg, axis=-1, keepdims=True)
    i1 = jnp.min(jnp.where(lg == m1, lane, LANES), axis=-1, keepdims=True)
    lg2 = jnp.where(lane == i1, neg, lg)
    m2 = jnp.max(lg2, axis=-1, keepdims=True)
    i2 = jnp.min(jnp.where(lg2 == m2, lane, LANES), axis=-1, keepdims=True)
    e2 = jnp.exp(m2 - m1)
    den = 1.0 + e2
    gates = jnp.where(lane == i1, 1.0 / den, 0.0) + jnp.where(lane == i2, e2 / den, 0.0)
    return gates, (lane == i1) | (lane == i2)


def _mix_kernel(att_ref, ys_ref, gate_ref, x_ref, ada_ref, wpa_ref, wglu_ref, bglu_ref, wps_ref,
                wout_ref, gffn_ref, wr_ref, br_ref, xo_ref, h2_ref, *route_refs, d, tm, group):
    ada = ada_ref[0]
    ya = _dot(att_ref[0], wpa_ref[...])
    ys = ys_ref[0]
    glu = ys * jax.nn.sigmoid(_dot(ys.astype(BF16), wglu_ref[...]) + bglu_ref[...])
    yp = _dot(glu.astype(BF16), wps_ref[...])
    mixed = gate_ref[0, :, 0:d].astype(F32) * ya + gate_ref[0, :, d:2 * d].astype(F32) * yp
    x_new = x_ref[0] + ada[2:3] * _dot(mixed.astype(BF16), wout_ref[...])
    xo_ref[0] = x_new
    h2 = _modulated_norm(x_new, gffn_ref[...], ada[3:4], ada[4:5])
    h2b = h2.astype(BF16)
    h2_ref[0] = h2b
    if not route_refs:
        return
    rkr_ref, gwr_ref, cnt_ref = route_refs
    h2l = (h2 - h2b.astype(F32)).astype(BF16)
    both = _dot(h2b, wr_ref[...])
    logits = (both[:, 0:LANES] + both[:, LANES:2 * LANES] + _dot(h2l, wr_ref[:, 0:LANES])) + br_ref[...]
    gates, mask = _route(logits)

    @pl.when(pl.program_id(1) % group == 0)
    def _():
        cnt_ref[...] = jnp.zeros_like(cnt_ref)

    row = lax.broadcasted_iota(jnp.int32, (tm, tm), 0)
    col = lax.broadcasted_iota(jnp.int32, (tm, tm), 1)
    maskf = mask.astype(F32)
    before = _dot((row > col).astype(BF16), maskf.astype(BF16)) + cnt_ref[0:1, :]
    rank = jnp.where(mask, before, -1.0)
    cnt_ref[...] = jnp.broadcast_to(before[tm - 1:tm, :] + maskf[tm - 1:tm, :], cnt_ref.shape)
    rkr_ref[...] = jnp.transpose(rank)[0:N_EXPERTS, :]
    gwr_ref[...] = jnp.transpose(gates)[0:N_EXPERTS, :]


def _mix(att, ys, gates, x, ada_l, wpa, wglu, bglu, wps, wout, gffn, wr_split, b_router, *, tm, group):
    bsz, seq, d = x.shape
    aw = att.shape[-1]
    sw = ys.shape[-1]
    nj = seq // tm
    kern = functools.partial(_mix_kernel, d=d, tm=tm, group=group)
    const = lambda b, j: (0, 0)
    tile = lambda b, j: (b, j, 0)
    out_specs = [pl.BlockSpec((1, tm, d), tile), pl.BlockSpec((1, tm, d), tile)]
    out_shape = [jax.ShapeDtypeStruct((bsz, seq, d), F32), jax.ShapeDtypeStruct((bsz, seq, d), BF16)]
    scratch = []
    if group:
        flat = lambda b, j: (0, b * nj + j)
        out_specs += [pl.BlockSpec((N_EXPERTS, tm), flat), pl.BlockSpec((N_EXPERTS, tm), flat)]
        out_shape += [jax.ShapeDtypeStruct((N_EXPERTS, bsz * seq), F32),
                      jax.ShapeDtypeStruct((N_EXPERTS, bsz * seq), F32)]
        scratch = [pltpu.VMEM((SUB, LANES), F32)]
    return pl.pallas_call(
        kern,
        grid=(bsz, nj),
        in_specs=[pl.BlockSpec((1, tm, aw), tile),
                  pl.BlockSpec((1, tm, sw), tile),
                  pl.BlockSpec((1, tm, 2 * d), tile),
                  pl.BlockSpec((1, tm, d), tile),
                  pl.BlockSpec((1, 6, d), lambda b, j: (b, 0, 0)),
                  pl.BlockSpec((aw, d), const),
                  pl.BlockSpec((sw, sw), const),
                  pl.BlockSpec((1, sw), const),
                  pl.BlockSpec((sw, d), const),
                  pl.BlockSpec((d, d), const),
                  pl.BlockSpec((1, d), const),
                  pl.BlockSpec((d, 2 * LANES), const),
                  pl.BlockSpec((1, LANES), const)],
        out_specs=out_specs,
        out_shape=out_shape,
        scratch_shapes=scratch,
        compiler_params=_params(("arbitrary", "arbitrary")),
        name="mix",
    )(att, ys, gates, x, ada_l, wpa, wglu, bglu, wps, wout, gffn, wr_split, b_router)


def _residual_out(x, ada, acc, gfin_ref, final_norm):
    y = x + ada[5:6] * acc
    if final_norm:
        y = y * lax.rsqrt(jnp.mean(y * y, axis=-1, keepdims=True) + EPS) * gfin_ref[...]
    return y


def _ffn_kernel(h_ref, x_ref, ada_ref, w1_ref, w3_ref, w2_ref, gfin_ref, o_ref, acc_ref, *, nf, final_norm):
    f = pl.program_id(1)

    @pl.when(f == 0)
    def _():
        acc_ref[...] = jnp.zeros_like(acc_ref)

    h = h_ref[...]
    act = jax.nn.silu(_dot(h, w1_ref[...])) * _dot(h, w3_ref[...])
    acc_ref[...] += _dot(act.astype(BF16), w2_ref[...])

    @pl.when(f == nf - 1)
    def _():
        o_ref[...] = _residual_out(x_ref[...], ada_ref[0], acc_ref[...], gfin_ref, final_norm)


def _ffn(h2, x, ada_l, w1, w3, w2, g_final, *, tm, cf, seq, final_norm):
    t, d = h2.shape
    dff = w1.shape[-1]
    nf = dff // cf
    tiles_per_seq = seq // tm
    kern = functools.partial(_ffn_kernel, nf=nf, final_norm=final_norm)
    return pl.pallas_call(
        kern,
        grid=(t // tm, nf),
        in_specs=[pl.BlockSpec((tm, d), lambda i, f: (i, 0)),
                  pl.BlockSpec((tm, d), lambda i, f: (i, 0)),
                  pl.BlockSpec((1, 6, d), lambda i, f: (i // tiles_per_seq, 0, 0)),
                  pl.BlockSpec((d, cf), lambda i, f: (0, f)),
                  pl.BlockSpec((d, cf), lambda i, f: (0, f)),
                  pl.BlockSpec((cf, d), lambda i, f: (f, 0)),
                  pl.BlockSpec((1, d), lambda i, f: (0, 0))],
        out_specs=pl.BlockSpec((tm, d), lambda i, f: (i, 0)),
        out_shape=jax.ShapeDtypeStruct((t, d), F32),
        scratch_shapes=[pltpu.VMEM((tm, d), F32)],
        compiler_params=_params(("arbitrary", "arbitrary")),
        name="ffn",
    )(h2, x, ada_l, w1, w3, w2, g_final)


def _moe_kernel(h_ref, rkr_ref, gwr_ref, x_ref, ada_ref, w1_ref, w3_ref, w2_ref, gfin_ref, o_ref,
                *, n_exp, rows, final_norm):
    e = pl.program_id(1)
    tf = h_ref.shape[0]

    @pl.when(e == 0)
    def _():
        o_ref[...] = jnp.zeros_like(o_ref)

    rk_row = rkr_ref[pl.ds(e, 1), :]
    gw_row = gwr_ref[pl.ds(e, 1), :]
    count = jnp.max(rk_row).astype(jnp.int32) + 1

    r_sub = lax.broadcasted_iota(jnp.int32, (rows, tf), 0).astype(F32)

    def chunk(c, carry):
        pick = rk_row == r_sub + (c * rows).astype(F32)
        pickb = pick.astype(BF16)
        xc = _dot(pickb, h_ref[...]).astype(BF16)
        gate = jnp.sum(jnp.where(pick, gw_row, 0.0), axis=-1, keepdims=True)
        act = jax.nn.silu(_dot(xc, w1_ref[0])) * _dot(xc, w3_ref[0]) * gate
        yc = _dot(act.astype(BF16), w2_ref[0]).astype(BF16)
        o_ref[...] += lax.dot_general(pickb, yc, (((0,), (0,)), ((), ())), preferred_element_type=F32)
        return carry

    lax.fori_loop(0, (count + rows - 1) // rows, chunk, 0)

    @pl.when(e == n_exp - 1)
    def _():
        o_ref[...] = _residual_out(x_ref[...], ada_ref[0], o_ref[...], gfin_ref, final_norm)


def _moe(h2, rk_row, gw_row, x, ada_l, w1, w3, w2, g_final, *, tf, rows, seq, final_norm):
    t, d = h2.shape
    n_exp, _, dff = w1.shape
    tiles_per_seq = seq // tf
    kern = functools.partial(_moe_kernel, n_exp=n_exp, rows=rows, final_norm=final_norm)
    return pl.pallas_call(
        kern,
        grid=(t // tf, n_exp),
        in_specs=[pl.BlockSpec((tf, d), lambda i, e: (i, 0)),
                  pl.BlockSpec((N_EXPERTS, tf), lambda i, e: (0, i)),
                  pl.BlockSpec((N_EXPERTS, tf), lambda i, e: (0, i)),
                  pl.BlockSpec((tf, d), lambda i, e: (i, 0)),
                  pl.BlockSpec((1, 6, d), lambda i, e: (i // tiles_per_seq, 0, 0)),
                  pl.BlockSpec((1, d, dff), lambda i, e: (e, 0, 0)),
                  pl.BlockSpec((1, d, dff), lambda i, e: (e, 0, 0)),
                  pl.BlockSpec((1, dff, d), lambda i, e: (e, 0, 0)),
                  pl.BlockSpec((1, d), lambda i, e: (0, 0))],
        out_specs=pl.BlockSpec((tf, d), lambda i, e: (i, 0)),
        out_shape=jax.ShapeDtypeStruct((t, d), F32),
        compiler_params=_params(("arbitrary", "arbitrary")),
        name="moe",
    )(h2, rk_row, gw_row, x, ada_l, w1, w3, w2, g_final)


def _tiles(seq, bsz):
    ti = min(256, seq)
    tm = min(1024, seq)
    tq = min(1024, seq // 2)
    tf = min(1024, seq)
    return ti, tm, tq, min(4, bsz), tf, 144


def kernel(x, c, w_ada, b_ada, g_mix, w_in, b_forget, b_gate, lam_re, lam_im, log_dt, b_re, b_im,
           c_re, c_im, d_skip, w_glu, b_glu, w_proj_att, w_proj_ssm, w_out, g_ffn, w1_dense, w3_dense,
           w2_dense, w_router, b_router, w1_moe, w3_moe, w2_moe, g_final):
    bsz, seq, d = x.shape
    depth = w_ada.shape[0]
    att_w = ATT_HEADS * ATT_HEAD_DIM
    ssm_w = w_glu.shape[-1]
    dff = w1_dense.shape[-1]
    ti, tm, tq, bb, tf, rows = _tiles(seq, bsz)
    cf = dff // 2

    ada = _ada(c, w_ada, b_ada).reshape(depth, bsz, 6, d)
    gfin = g_final.reshape(1, d)

    for l in range(depth):
        o1, o2, o3 = 3 * att_w, 3 * att_w + ATT_HEADS, 3 * att_w + ATT_HEADS + ssm_w
        w = w_in[l]
        w_fg = jnp.pad(w[:, o1:o2], ((0, 0), (0, LANES - ATT_HEADS)))
        w_all = jnp.concatenate([w[:, :o1], w[:, o2:o3], w[:, o3:], w_fg], axis=1).astype(BF16)
        bf_pad = jnp.pad(b_forget[l], (0, LANES - ATT_HEADS)).reshape(1, LANES)

        qkv, u, gates, qaug, kaug = _inproj(x, ada[l], g_mix[l].reshape(1, d), w_all,
                                      b_gate[l].reshape(1, 2 * d), bf_pad,
                                      tm=ti, att_w=att_w, ssm_w=ssm_w)

        att = _attention(qkv, qaug, kaug, tq=tq)

        s5w = _s5_operators(lam_re[l], lam_im[l], log_dt[l], b_re[l], b_im[l], c_re[l], c_im[l])
        ys = _s5(u, (*s5w, d_skip[l].astype(F32).reshape(-1, 1, LANES)), bb=bb)

        moe = l % 2 == 1
        m = l // 2
        if moe:
            wr = jnp.pad(w_router[m], ((0, 0), (0, LANES - N_EXPERTS)))
            br = jnp.pad(b_router[m], (0, LANES - N_EXPERTS)).reshape(1, LANES)
        else:
            wr = jnp.zeros((d, LANES), F32)
            br = jnp.zeros((1, LANES), F32)
        wr_hi = wr.astype(BF16)
        wr_split = jnp.concatenate([wr_hi, (wr - wr_hi.astype(F32)).astype(BF16)], axis=1)
        mixed = _mix(att, ys, gates, x, ada[l],
                     w_proj_att[l].astype(BF16), w_glu[l].astype(BF16),
                     b_glu[l].reshape(1, ssm_w), w_proj_ssm[l].astype(BF16),
                     w_out[l].astype(BF16), g_ffn[l].reshape(1, d), wr_split, br,
                     tm=tm, group=(tf // tm if moe else 0))
        x, h2 = mixed[0].reshape(bsz * seq, d), mixed[1].reshape(bsz * seq, d)
        last = l == depth - 1
        if moe:
            x = _moe(h2, mixed[2], mixed[3], x, ada[l],
                     w1_moe[m].astype(BF16), w3_moe[m].astype(BF16), w2_moe[m].astype(BF16), gfin,
                     tf=tf, rows=rows, seq=seq, final_norm=last)
        else:
            x = _ffn(h2, x, ada[l], w1_dense[m].astype(BF16), w3_dense[m].astype(BF16),
                     w2_dense[m].astype(BF16), gfin, tm=tf, cf=cf, seq=seq, final_norm=last)
        x = x.reshape(bsz, seq, d)
    return x
```

```python
import functools
import math

import jax
import jax.numpy as jnp
from jax import lax
from jax.experimental import pallas as pl
from jax.experimental.pallas import tpu as pltpu

F32 = jnp.float32
BF16 = jnp.bfloat16
EPS = 1e-6

ATT_HEADS = 8
ATT_HEAD_DIM = 64
SSM_GROUP = 16
SSM_STATE = 64
N_EXPERTS = 8
LANES = 128
SUB = 8
LOG2E = math.log2(math.e)
F_OFF = 32
VMEM_LIMIT = 56 * 1024 * 1024


def _params(sem):
    return pltpu.CompilerParams(dimension_semantics=sem, vmem_limit_bytes=VMEM_LIMIT)


def _dot(a, b):
    return jnp.dot(a, b, preferred_element_type=F32)


def _ada_kernel(c_ref, w_ref, b_ref, o_ref):
    cond = jax.nn.silu(c_ref[...])
    o_ref[0] = jnp.dot(cond, w_ref[0], preferred_element_type=F32,
                       precision=lax.Precision.HIGHEST) + b_ref[0]


def _ada(c, w_ada, b_ada, tn=1536):
    depth, d, n = w_ada.shape
    bsz = c.shape[0]
    return pl.pallas_call(
        _ada_kernel,
        grid=(depth, n // tn),
        in_specs=[pl.BlockSpec((bsz, d), lambda l, j: (0, 0)),
                  pl.BlockSpec((1, d, tn), lambda l, j: (l, 0, j)),
                  pl.BlockSpec((1, 1, tn), lambda l, j: (l, 0, j))],
        out_specs=pl.BlockSpec((1, bsz, tn), lambda l, j: (l, 0, j)),
        out_shape=jax.ShapeDtypeStruct((depth, bsz, n), F32),
        compiler_params=_params(("arbitrary", "arbitrary")),
        name="ada",
    )(c, w_ada, b_ada.reshape(depth, 1, n))


def _modulated_norm(x, g, shift, scale):
    y = x * lax.rsqrt(jnp.mean(x * x, axis=-1, keepdims=True) + EPS) * g
    return y * (1.0 + scale) + shift


def _inproj_kernel(x_ref, ada_ref, g_ref, w_ref, bg_ref, bf_ref,
                   qkv_ref, u_ref, gate_ref, qa_ref, ka_ref, carry_ref, *, tm, att_w, ssm_w, d):
    j = pl.program_id(1)
    ada = ada_ref[0]
    h = _modulated_norm(x_ref[0], g_ref[...], ada[0:1], ada[1:2]).astype(BF16)

    cw = att_w
    qkv_ref[0, :, 0:cw] = (_dot(h, w_ref[:, 0:cw]) * (ATT_HEAD_DIM ** -0.5 * LOG2E)).astype(BF16)
    for c0 in range(cw, 3 * att_w, cw):
        qkv_ref[0, :, c0:c0 + cw] = _dot(h, w_ref[:, c0:c0 + cw]).astype(BF16)
    o_u = 3 * att_w
    u_ref[0] = _dot(h, w_ref[:, o_u:o_u + ssm_w])
    o_g = o_u + ssm_w
    for c0 in range(0, 2 * d, cw):
        z = _dot(h, w_ref[:, o_g + c0:o_g + c0 + cw]) + bg_ref[:, c0:c0 + cw]
        gate_ref[0, :, c0:c0 + cw] = jax.nn.sigmoid(z).astype(BF16)

    o_f = o_g + 2 * d
    logf = jax.nn.log_sigmoid(_dot(h, w_ref[:, o_f:o_f + LANES]) + bf_ref[...])
    row = lax.broadcasted_iota(jnp.int32, (tm, tm), 0)
    col = lax.broadcasted_iota(jnp.int32, (tm, tm), 1)
    tri = (row >= col).astype(BF16)
    hi = logf.astype(BF16)
    r1 = logf - hi.astype(F32)
    mid = r1.astype(BF16)
    lo = (r1 - mid.astype(F32)).astype(BF16)
    cs = _dot(tri, hi) + _dot(tri, mid) + _dot(tri, lo)

    @pl.when(j == 0)
    def _():
        carry_ref[...] = jnp.zeros_like(carry_ref)

    fcum = cs + carry_ref[0:1, :]
    carry_ref[...] = jnp.broadcast_to(fcum[tm - 1:tm, :], carry_ref.shape)

    lane = lax.broadcasted_iota(jnp.int32, (tm, LANES), 1)
    fl = jnp.where(lane < ATT_HEADS, fcum * LOG2E, 0.0)
    p0 = fl.astype(BF16).astype(F32)
    r0 = fl - p0
    p1 = r0.astype(BF16).astype(F32)
    p2 = (r0 - p1).astype(BF16).astype(F32)
    comb = p0 + pltpu.roll(p1, ATT_HEADS, 1) + pltpu.roll(p2, 2 * ATT_HEADS, 1)
    sel_lo = (lane < 3 * ATT_HEADS).astype(F32)
    sel_hi = ((lane >= F_OFF) & (lane < F_OFF + 3 * ATT_HEADS)).astype(F32)
    qa_ref[0] = (pltpu.roll(comb, F_OFF, 1) + sel_lo).astype(BF16)
    ka_ref[0] = (sel_hi - comb).astype(BF16)


def _inproj(x, ada_l, g, w_all, b_gate, b_forget_pad, *, tm, att_w, ssm_w):
    bsz, seq, d = x.shape
    n_all = w_all.shape[1]
    kern = functools.partial(_inproj_kernel, tm=tm, att_w=att_w, ssm_w=ssm_w, d=d)
    return pl.pallas_call(
        kern,
        grid=(bsz, seq // tm),
        in_specs=[pl.BlockSpec((1, tm, d), lambda b, j: (b, j, 0)),
                  pl.BlockSpec((1, 6, d), lambda b, j: (b, 0, 0)),
                  pl.BlockSpec((1, d), lambda b, j: (0, 0)),
                  pl.BlockSpec((d, n_all), lambda b, j: (0, 0)),
                  pl.BlockSpec((1, 2 * d), lambda b, j: (0, 0)),
                  pl.BlockSpec((1, LANES), lambda b, j: (0, 0))],
        out_specs=[pl.BlockSpec((1, tm, 3 * att_w), lambda b, j: (b, j, 0)),
                   pl.BlockSpec((1, tm, ssm_w), lambda b, j: (b, j, 0)),
                   pl.BlockSpec((1, tm, 2 * d), lambda b, j: (b, j, 0)),
                   pl.BlockSpec((1, tm, LANES), lambda b, j: (b, j, 0)),
                   pl.BlockSpec((1, tm, LANES), lambda b, j: (b, j, 0))],
        out_shape=[jax.ShapeDtypeStruct((bsz, seq, 3 * att_w), BF16),
                   jax.ShapeDtypeStruct((bsz, seq, ssm_w), F32),
                   jax.ShapeDtypeStruct((bsz, seq, 2 * d), BF16),
                   jax.ShapeDtypeStruct((bsz, seq, LANES), BF16),
                   jax.ShapeDtypeStruct((bsz, seq, LANES), BF16)],
        scratch_shapes=[pltpu.VMEM((8, LANES), F32)],
        compiler_params=_params(("arbitrary", "arbitrary")),
        name="inproj",
    )(x, ada_l, g, w_all, b_gate, b_forget_pad)


def _attn_kernel(q_ref, qa_ref, k_ref, ka_ref, v_ref, o_ref, *, tq):
    hp = pl.program_id(1)
    i = pl.program_id(2)
    half = tq // 2
    lane = lax.broadcasted_iota(jnp.int32, (tq, LANES), 1)
    first = lane < ATT_HEAD_DIM
    q2 = q_ref[0]
    qa = qa_ref[0]
    zero = jnp.zeros_like(q2)
    heads = []
    for hh in range(2):
        qsel = jnp.where(first, q2, zero) if hh == 0 else jnp.where(first, zero, q2)
        asel = jnp.where((lane & (ATT_HEADS - 1)) == 2 * hp + hh, qa, zero)
        heads.append(jnp.concatenate([qsel, asel], axis=1))

    def kv(start, size):
        kj = jnp.concatenate([k_ref[0, pl.ds(start, size), :], ka_ref[0, pl.ds(start, size), :]], axis=1)
        v2 = v_ref[0, pl.ds(start, size), :]
        one = jnp.ones_like(v2)
        lk = lax.broadcasted_iota(jnp.int32, v2.shape, 1) < ATT_HEAD_DIM
        return kj, (jnp.where(lk, v2, one), jnp.where(lk, one, v2))

    def update(qh, kj, vh, m, acc, mask):
        s = lax.dot_general(qh, kj, (((1,), (1,)), ((), ())), preferred_element_type=F32)
        if mask is not None:
            s = jnp.where(mask, s, -jnp.inf)
        m_new = jnp.maximum(m, jnp.max(s, axis=-1, keepdims=True))
        p = jnp.exp2(s - m_new).astype(BF16)
        acc = jnp.exp2(m - m_new) * acc + _dot(p, vh)
        return m_new, acc

    def body(j, carry):
        kj, vs = kv(pl.multiple_of(j * tq, tq), tq)
        out = []
        for hh in range(2):
            out.extend(update(heads[hh], kj, vs[hh], carry[2 * hh], carry[2 * hh + 1], None))
        return tuple(out)

    init = []
    for hh in range(2):
        init += [jnp.full((tq, 1), -jnp.inf, F32), jnp.zeros((tq, LANES), F32)]
    carry = lax.fori_loop(0, i, body, tuple(init))

    base = pl.multiple_of(i * tq, tq)
    r_top = lax.broadcasted_iota(jnp.int32, (half, half), 0)
    c_top = lax.broadcasted_iota(jnp.int32, (half, half), 1)
    r_bot = lax.broadcasted_iota(jnp.int32, (half, tq), 0)
    c_bot = lax.broadcasted_iota(jnp.int32, (half, tq), 1)
    k_top, v_top = kv(base, half)
    k_all, v_all = kv(base, tq)
    outs = []
    for hh in range(2):
        m, acc = carry[2 * hh], carry[2 * hh + 1]
        qh = heads[hh]
        _, acc_t = update(qh[:half], k_top, v_top[hh], m[:half], acc[:half], r_top >= c_top)
        _, acc_b = update(qh[half:], k_all, v_all[hh], m[half:], acc[half:], r_bot + half >= c_bot)
        acc = jnp.concatenate([acc_t, acc_b], axis=0)
        outs.append(acc / pltpu.roll(acc, ATT_HEAD_DIM, 1))
    o_ref[0] = jnp.where(first, outs[0], outs[1]).astype(BF16)


def _attention(qkv, qaug, kaug, *, tq):
    bsz, seq, w3 = qkv.shape
    att_w = w3 // 3
    npair = att_w // LANES
    kern = functools.partial(_attn_kernel, tq=tq)
    return pl.pallas_call(
        kern,
        grid=(bsz, npair, seq // tq),
        in_specs=[pl.BlockSpec((1, tq, LANES), lambda b, p, i: (b, i, p)),
                  pl.BlockSpec((1, tq, LANES), lambda b, p, i: (b, i, 0)),
                  pl.BlockSpec((1, seq, LANES), lambda b, p, i: (b, 0, npair + p)),
                  pl.BlockSpec((1, seq, LANES), lambda b, p, i: (b, 0, 0)),
                  pl.BlockSpec((1, seq, LANES), lambda b, p, i: (b, 0, 2 * npair + p))],
        out_specs=pl.BlockSpec((1, tq, LANES), lambda b, p, i: (b, i, p)),
        out_shape=jax.ShapeDtypeStruct((bsz, seq, att_w), BF16),
        compiler_params=_params(("arbitrary", "arbitrary", "arbitrary")),
        name="fox_attention",
    )(qkv, qaug, qkv, kaug, qkv)


def _split(x):
    hi = x.astype(BF16)
    return hi, (x - hi.astype(F32)).astype(BF16)


def _dot_nt3(a, b):
    dims = (((1,), (1,)), ((), ()))
    ah, al = _split(a)
    bh, bl = _split(b)
    nt = lambda u, v: lax.dot_general(u, v, dims, preferred_element_type=F32)
    return nt(ah, bh) + nt(ah, bl) + nt(al, bh)


def _dot_rep3(x, rep):
    p0 = x.astype(BF16)
    r0 = x - p0.astype(F32)
    p1 = r0.astype(BF16)
    p2 = (r0 - p1.astype(F32)).astype(BF16)
    return _dot(p0, rep) + _dot(p1, rep) + _dot(p2, rep)


def _s5_operator_kernel(lr_ref, li_ref, dt_ref, br_ref, bi_ref, cr_ref, ci_ref,
                        t_ref, b_ref, ct_ref, a1_ref, a2_ref, *, hc, p):
    gl = LANES // hc
    half = gl * p
    lr, li, dt = lr_ref[0], li_ref[0], jnp.exp(dt_ref[0])
    br, bi, cr, ci = br_ref[0], bi_ref[0], cr_ref[0], ci_ref[0]
    mag = jnp.exp(lr * dt)
    ang = li * dt
    a_re, a_im = mag * jnp.cos(ang), mag * jnp.sin(ang)
    n_re, n_im = a_re - 1.0, a_im
    den = lr * lr + li * li
    k_re = (n_re * lr + n_im * li) / den
    k_im = (n_im * lr - n_re * li) / den
    bb_re = k_re * br - k_im * bi
    bb_im = k_re * bi + k_im * br

    def power(t):
        m = jnp.exp(lr * dt * t)
        return m * jnp.cos(ang * t), m * jnp.sin(ang * t)

    pw = [power(float(t)) for t in range(SUB + 1)]

    rep = (lax.broadcasted_iota(jnp.int32, (p, half), 1) % p
           == lax.broadcasted_iota(jnp.int32, (p, half), 0)).astype(BF16)
    own = (lax.broadcasted_iota(jnp.int32, (LANES, half), 0) // hc
           == lax.broadcasted_iota(jnp.int32, (LANES, half), 1) // p)

    def expand(x):
        return jnp.where(own, _dot(x.astype(BF16), rep), 0.0).astype(BF16)

    same = (lax.broadcasted_iota(jnp.int32, (LANES, LANES), 0) // hc
            == lax.broadcasted_iota(jnp.int32, (LANES, LANES), 1) // hc)
    zero = jnp.zeros((LANES, LANES), BF16)
    kern = []
    for t in range(SUB):
        pr, pi = pw[t]
        wr, wi = cr * pr - ci * pi, cr * pi + ci * pr
        k = _dot_nt3(bb_re, wr) - _dot_nt3(bb_im, wi)
        kern.append(jnp.where(same, k, 0.0).astype(BF16))
    for j in range(SUB):
        for i in range(SUB):
            t_ref[0, j * LANES:(j + 1) * LANES, i * LANES:(i + 1) * LANES] = kern[i - j] if i >= j else zero
        pr, pi = pw[SUB - 1 - j]
        b_ref[0, j * LANES:(j + 1) * LANES, 0:half] = expand(pr * bb_re - pi * bb_im)
        b_ref[0, j * LANES:(j + 1) * LANES, half:2 * half] = expand(pr * bb_im + pi * bb_re)
        pr, pi = pw[j + 1]
        ct_ref[0, j * LANES:(j + 1) * LANES, 0:half] = expand(cr * pr - ci * pi)
        ct_ref[0, j * LANES:(j + 1) * LANES, half:2 * half] = expand(-(cr * pi + ci * pr))
    pr, pi = pw[SUB]
    along = lambda x: jnp.sum(jnp.where(own, _dot_rep3(x, rep), 0.0), axis=0, keepdims=True) * (1.0 / hc)
    ac_re, ac_im = along(pr), along(pi)
    a1_ref[0] = jnp.concatenate([ac_re, ac_re], axis=1)
    a2_ref[0] = jnp.concatenate([-ac_im, ac_im], axis=1)


def _s5_operators(lam_re, lam_im, log_dt, b_re, b_im, c_re, c_im):
    g, p = lam_re.shape
    hc = b_re.shape[-1]
    nt = g * hc // LANES
    ch = SUB * LANES
    ns = 2 * (LANES // hc) * p
    rows = lambda v: jnp.broadcast_to(v[:, None, :], (g, hc, v.shape[-1])).reshape(nt, LANES, v.shape[-1])
    compact = [rows(lam_re.astype(F32)), rows(lam_im.astype(F32)),
               rows(jnp.broadcast_to(log_dt.astype(F32)[:, None], (g, p))),
               b_re.astype(F32).transpose(0, 2, 1).reshape(nt, LANES, p),
               b_im.astype(F32).transpose(0, 2, 1).reshape(nt, LANES, p),
               c_re.astype(F32).reshape(nt, LANES, p), c_im.astype(F32).reshape(nt, LANES, p)]
    kern = functools.partial(_s5_operator_kernel, hc=hc, p=p)
    return pl.pallas_call(
        kern,
        grid=(nt,),
        in_specs=[pl.BlockSpec((1, LANES, p), lambda t: (t, 0, 0))] * 7,
        out_specs=[pl.BlockSpec((1, ch, ch), lambda t: (t, 0, 0)),
                   pl.BlockSpec((1, ch, ns), lambda t: (t, 0, 0)),
                   pl.BlockSpec((1, ch, ns), lambda t: (t, 0, 0)),
                   pl.BlockSpec((1, 1, ns), lambda t: (t, 0, 0)),
                   pl.BlockSpec((1, 1, ns), lambda t: (t, 0, 0))],
        out_shape=[jax.ShapeDtypeStruct((nt, ch, ch), BF16), jax.ShapeDtypeStruct((nt, ch, ns), BF16),
                   jax.ShapeDtypeStruct((nt, ch, ns), BF16), jax.ShapeDtypeStruct((nt, 1, ns), F32),
                   jax.ShapeDtypeStruct((nt, 1, ns), F32)],
        compiler_params=_params(("arbitrary",)),
        name="s5_operators",
    )(*compact)


def _s5_kernel(u_ref, t_ref, b_ref, ct_ref, a1_ref, a2_ref, d_ref, y_ref, uf_ref, s_ref, *, bb, nk):
    nc = s_ref.shape[0]
    for b in range(bb):
        for j in range(SUB):
            uf_ref[b * nk:(b + 1) * nk, j * LANES:(j + 1) * LANES] = (
                u_ref[b, pl.ds(j, nk, stride=SUB), :].astype(BF16))
    pitch = s_ref.shape[1] // bb
    for b in range(bb):
        s_loc = _dot(uf_ref[b * nk:(b + 1) * nk, :], b_ref[0])
        for c in range(nc):
            s_ref[c, b * pitch:b * pitch + nk, :] = s_loc[:, c * LANES:(c + 1) * LANES]
    a1 = a1_ref[0]
    a2 = a2_ref[0]

    def carry_step(k, state):
        rows = pl.ds(k, bb, stride=pitch)
        loc = jnp.concatenate([s_ref[c, rows, :] for c in range(nc)], axis=1)
        for c in range(nc):
            s_ref[c, rows, :] = state[:, c * LANES:(c + 1) * LANES]
        return a1 * state + a2 * pltpu.roll(state, nc * LANES // 2, 1) + loc

    lax.fori_loop(0, nk, carry_step, jnp.zeros((bb, nc * LANES), F32), unroll=4)
    d = d_ref[0]
    for b in range(bb):
        rows = slice(b * nk, (b + 1) * nk)
        s_prev = jnp.concatenate([s_ref[c, b * pitch:b * pitch + nk, :] for c in range(nc)],
                                 axis=1).astype(BF16)
        y = _dot(uf_ref[rows, :], t_ref[0]) + lax.dot_general(
            s_prev, ct_ref[0], (((1,), (1,)), ((), ())), preferred_element_type=F32)
        for i in range(SUB):
            yi = y[:, i * LANES:(i + 1) * LANES] + d * u_ref[b, pl.ds(i, nk, stride=SUB), :]
            y_ref[b, pl.ds(i, nk, stride=SUB), :] = jax.nn.gelu(yi)


def _s5(u, weights, *, bb):
    bsz, seq, w = u.shape
    t_intra, b_state, c_state, a1, a2, dvec = weights
    nt = t_intra.shape[0]
    nk = seq // SUB
    ch = SUB * LANES
    ns = b_state.shape[-1]
    kern = functools.partial(_s5_kernel, bb=bb, nk=nk)
    return pl.pallas_call(
        kern,
        grid=(nt, bsz // bb),
        in_specs=[pl.BlockSpec((bb, seq, LANES), lambda t, i: (i, 0, t)),
                  pl.BlockSpec((1, ch, ch), lambda t, i: (t, 0, 0)),
                  pl.BlockSpec((1, ch, ns), lambda t, i: (t, 0, 0)),
                  pl.BlockSpec((1, ns, ch), lambda t, i: (t, 0, 0)),
                  pl.BlockSpec((1, 1, ns), lambda t, i: (t, 0, 0)),
                  pl.BlockSpec((1, 1, ns), lambda t, i: (t, 0, 0)),
                  pl.BlockSpec((1, 1, LANES), lambda t, i: (t, 0, 0))],
        out_specs=pl.BlockSpec((bb, seq, LANES), lambda t, i: (i, 0, t)),
        out_shape=jax.ShapeDtypeStruct((bsz, seq, w), F32),
        scratch_shapes=[pltpu.VMEM((bb * nk, ch), BF16),
                        pltpu.VMEM((ns // LANES, bb * (nk + SUB * (1 + nk // SUB % 2)), LANES), F32)],
        compiler_params=_params(("arbitrary", "arbitrary")),
        name="s5_chunked",
    )(u, t_intra, b_state, c_state, a1, a2, dvec)


def _route(lg):
    n = lg.shape[0]
    sub = lax.broadcasted_iota(jnp.int32, lg.shape, 0)
    neg = jnp.float32(-jnp.inf)
    m1 = jnp.max(lg, axis=0, keepdims=True)
    i1 = jnp.min(jnp.where(lg == m1, sub, n), axis=0, keepdims=True)
    lg2 = jnp.where(sub == i1, neg, lg)
    m2 = jnp.max(lg2, axis=0, keepdims=True)
    i2 = jnp.min(jnp.where(lg2 == m2, sub, n), axis=0, keepdims=True)
    e2 = jnp.exp(m2 - m1)
    den = 1.0 + e2
    gates = jnp.where(sub == i1, 1.0 / den, 0.0) + jnp.where(sub == i2, e2 / den, 0.0)
    return gates, (sub == i1) | (sub == i2)


def _mix_kernel(att_ref, ys_ref, gate_ref, x_ref, ada_ref, wpa_ref, wglu_ref, bglu_ref, wps_ref,
                wout_ref, gffn_ref, wr_ref, br_ref, xo_ref, h2_ref, *route_refs, d, tm, group):
    ada = ada_ref[0]
    ya = _dot(att_ref[0], wpa_ref[...])
    ys = ys_ref[0]
    glu = ys * jax.nn.sigmoid(_dot(ys.astype(BF16), wglu_ref[...]) + bglu_ref[...])
    yp = _dot(glu.astype(BF16), wps_ref[...])
    mixed = gate_ref[0, :, 0:d].astype(F32) * ya + gate_ref[0, :, d:2 * d].astype(F32) * yp
    x_new = x_ref[0] + ada[2:3] * _dot(mixed.astype(BF16), wout_ref[...])
    xo_ref[0] = x_new
    h2 = _modulated_norm(x_new, gffn_ref[...], ada[3:4], ada[4:5])
    h2b = h2.astype(BF16)
    h2_ref[0] = h2b
    if not route_refs:
        return
    rkr_ref, gwr_ref, cnt_ref = route_refs
    h2l = (h2 - h2b.astype(F32)).astype(BF16)
    both = _dot(h2b, wr_ref[...])
    logits = (both[:, 0:LANES] + both[:, LANES:2 * LANES] + _dot(h2l, wr_ref[:, 0:LANES])) + br_ref[...]
    gates, mask = _route(jnp.transpose(logits)[0:N_EXPERTS, :])

    @pl.when(pl.program_id(1) % group == 0)
    def _():
        cnt_ref[...] = jnp.zeros_like(cnt_ref)

    lane = lax.broadcasted_iota(jnp.int32, (N_EXPERTS, tm), 1)
    maskf = mask.astype(F32)
    incl = maskf
    shift = 1
    while shift < tm:
        incl = incl + jnp.where(lane >= shift, pltpu.roll(incl, shift, 1), 0.0)
        shift *= 2
    before = incl - maskf + cnt_ref[:, 0:1]
    rkr_ref[...] = jnp.where(mask, before, -1.0)
    gwr_ref[...] = gates
    cnt_ref[...] = jnp.broadcast_to(before[:, tm - 1:tm] + maskf[:, tm - 1:tm], cnt_ref.shape)


def _mix(att, ys, gates, x, ada_l, wpa, wglu, bglu, wps, wout, gffn, wr_split, b_router, *, tm, group):
    bsz, seq, d = x.shape
    aw = att.shape[-1]
    sw = ys.shape[-1]
    nj = seq // tm
    kern = functools.partial(_mix_kernel, d=d, tm=tm, group=group)
    const = lambda b, j: (0, 0)
    tile = lambda b, j: (b, j, 0)
    out_specs = [pl.BlockSpec((1, tm, d), tile), pl.BlockSpec((1, tm, d), tile)]
    out_shape = [jax.ShapeDtypeStruct((bsz, seq, d), F32), jax.ShapeDtypeStruct((bsz, seq, d), BF16)]
    scratch = []
    if group:
        flat = lambda b, j: (0, b * nj + j)
        out_specs += [pl.BlockSpec((N_EXPERTS, tm), flat), pl.BlockSpec((N_EXPERTS, tm), flat)]
        out_shape += [jax.ShapeDtypeStruct((N_EXPERTS, bsz * seq), F32),
                      jax.ShapeDtypeStruct((N_EXPERTS, bsz * seq), F32)]
        scratch = [pltpu.VMEM((N_EXPERTS, LANES), F32)]
    return pl.pallas_call(
        kern,
        grid=(bsz, nj),
        in_specs=[pl.BlockSpec((1, tm, aw), tile),
                  pl.BlockSpec((1, tm, sw), tile),
                  pl.BlockSpec((1, tm, 2 * d), tile),
                  pl.BlockSpec((1, tm, d), tile),
                  pl.BlockSpec((1, 6, d), lambda b, j: (b, 0, 0)),
                  pl.BlockSpec((aw, d), const),
                  pl.BlockSpec((sw, sw), const),
                  pl.BlockSpec((1, sw), const),
                  pl.BlockSpec((sw, d), const),
                  pl.BlockSpec((d, d), const),
                  pl.BlockSpec((1, d), const),
                  pl.BlockSpec((d, 2 * LANES), const),
                  pl.BlockSpec((1, LANES), const)],
        out_specs=out_specs,
        out_shape=out_shape,
        scratch_shapes=scratch,
        compiler_params=_params(("arbitrary", "arbitrary")),
        name="mix",
    )(att, ys, gates, x, ada_l, wpa, wglu, bglu, wps, wout, gffn, wr_split, b_router)


def _residual_out(x, ada, acc, gfin_ref, final_norm):
    y = x + ada[5:6] * acc
    if final_norm:
        y = y * lax.rsqrt(jnp.mean(y * y, axis=-1, keepdims=True) + EPS) * gfin_ref[...]
    return y


def _ffn_kernel(h_ref, x_ref, ada_ref, w1_ref, w3_ref, w2_ref, gfin_ref, o_ref, acc_ref, *, nf, final_norm):
    f = pl.program_id(1)

    @pl.when(f == 0)
    def _():
        acc_ref[...] = jnp.zeros_like(acc_ref)

    h = h_ref[...]
    act = jax.nn.silu(_dot(h, w1_ref[...])) * _dot(h, w3_ref[...])
    acc_ref[...] += _dot(act.astype(BF16), w2_ref[...])

    @pl.when(f == nf - 1)
    def _():
        o_ref[...] = _residual_out(x_ref[...], ada_ref[0], acc_ref[...], gfin_ref, final_norm)


def _ffn(h2, x, ada_l, w1, w3, w2, g_final, *, tm, cf, seq, final_norm):
    t, d = h2.shape
    dff = w1.shape[-1]
    nf = dff // cf
    tiles_per_seq = seq // tm
    kern = functools.partial(_ffn_kernel, nf=nf, final_norm=final_norm)
    return pl.pallas_call(
        kern,
        grid=(t // tm, nf),
        in_specs=[pl.BlockSpec((tm, d), lambda i, f: (i, 0)),
                  pl.BlockSpec((tm, d), lambda i, f: (i, 0)),
                  pl.BlockSpec((1, 6, d), lambda i, f: (i // tiles_per_seq, 0, 0)),
                  pl.BlockSpec((d, cf), lambda i, f: (0, f)),
                  pl.BlockSpec((d, cf), lambda i, f: (0, f)),
                  pl.BlockSpec((cf, d), lambda i, f: (f, 0)),
                  pl.BlockSpec((1, d), lambda i, f: (0, 0))],
        out_specs=pl.BlockSpec((tm, d), lambda i, f: (i, 0)),
        out_shape=jax.ShapeDtypeStruct((t, d), F32),
        scratch_shapes=[pltpu.VMEM((tm, d), F32)],
        compiler_params=_params(("arbitrary", "arbitrary")),
        name="ffn",
    )(h2, x, ada_l, w1, w3, w2, g_final)


def _moe_kernel(h_ref, rkr_ref, gwr_ref, x_ref, ada_ref, w1_ref, w3_ref, w2_ref, gfin_ref, o_ref,
                *, n_exp, rows, final_norm):
    e = pl.program_id(1)
    tf = h_ref.shape[0]

    @pl.when(e == 0)
    def _():
        o_ref[...] = jnp.zeros_like(o_ref)

    rk_row = rkr_ref[pl.ds(e, 1), :]
    gw_row = gwr_ref[pl.ds(e, 1), :]
    count = jnp.max(rk_row).astype(jnp.int32) + 1

    r_sub = lax.broadcasted_iota(jnp.int32, (rows, tf), 0).astype(F32)

    def chunk(c, carry):
        pick = rk_row == r_sub + (c * rows).astype(F32)
        pickb = pick.astype(BF16)
        xc = _dot(pickb, h_ref[...]).astype(BF16)
        gate = jnp.sum(jnp.where(pick, gw_row, 0.0), axis=-1, keepdims=True)
        act = jax.nn.silu(_dot(xc, w1_ref[0])) * _dot(xc, w3_ref[0]) * gate
        yc = _dot(act.astype(BF16), w2_ref[0]).astype(BF16)
        o_ref[...] += lax.dot_general(pickb, yc, (((0,), (0,)), ((), ())), preferred_element_type=F32)
        return carry

    lax.fori_loop(0, (count + rows - 1) // rows, chunk, 0)

    @pl.when(e == n_exp - 1)
    def _():
        o_ref[...] = _residual_out(x_ref[...], ada_ref[0], o_ref[...], gfin_ref, final_norm)


def _moe(h2, rk_row, gw_row, x, ada_l, w1, w3, w2, g_final, *, tf, rows, seq, final_norm):
    t, d = h2.shape
    n_exp, _, dff = w1.shape
    tiles_per_seq = seq // tf
    kern = functools.partial(_moe_kernel, n_exp=n_exp, rows=rows, final_norm=final_norm)
    return pl.pallas_call(
        kern,
        grid=(t // tf, n_exp),
        in_specs=[pl.BlockSpec((tf, d), lambda i, e: (i, 0)),
                  pl.BlockSpec((N_EXPERTS, tf), lambda i, e: (0, i)),
                  pl.BlockSpec((N_EXPERTS, tf), lambda i, e: (0, i)),
                  pl.BlockSpec((tf, d), lambda i, e: (i, 0)),
                  pl.BlockSpec((1, 6, d), lambda i, e: (i // tiles_per_seq, 0, 0)),
                  pl.BlockSpec((1, d, dff), lambda i, e: (e, 0, 0)),
                  pl.BlockSpec((1, d, dff), lambda i, e: (e, 0, 0)),
                  pl.BlockSpec((1, dff, d), lambda i, e: (e, 0, 0)),
                  pl.BlockSpec((1, d), lambda i, e: (0, 0))],
        out_specs=pl.BlockSpec((tf, d), lambda i, e: (i, 0)),
        out_shape=jax.ShapeDtypeStruct((t, d), F32),
        compiler_params=_params(("arbitrary", "arbitrary")),
        name="moe",
    )(h2, rk_row, gw_row, x, ada_l, w1, w3, w2, g_final)


def _tiles(seq, bsz):
    ti = min(256, seq)
    tm = min(1024, seq)
    tq = min(1024, seq // 2)
    tf = min(1024, seq)
    return ti, tm, tq, min(4, bsz), tf, 144


def kernel(x, c, w_ada, b_ada, g_mix, w_in, b_forget, b_gate, lam_re, lam_im, log_dt, b_re, b_im,
           c_re, c_im, d_skip, w_glu, b_glu, w_proj_att, w_proj_ssm, w_out, g_ffn, w1_dense, w3_dense,
           w2_dense, w_router, b_router, w1_moe, w3_moe, w2_moe, g_final):
    bsz, seq, d = x.shape
    depth = w_ada.shape[0]
    att_w = ATT_HEADS * ATT_HEAD_DIM
    ssm_w = w_glu.shape[-1]
    dff = w1_dense.shape[-1]
    ti, tm, tq, bb, tf, rows = _tiles(seq, bsz)
    cf = dff // 2

    ada = _ada(c, w_ada, b_ada).reshape(depth, bsz, 6, d)
    gfin = g_final.reshape(1, d)

    for l in range(depth):
        o1, o2, o3 = 3 * att_w, 3 * att_w + ATT_HEADS, 3 * att_w + ATT_HEADS + ssm_w
        w = w_in[l]
        w_fg = jnp.pad(w[:, o1:o2], ((0, 0), (0, LANES - ATT_HEADS)))
        w_all = jnp.concatenate([w[:, :o1], w[:, o2:o3], w[:, o3:], w_fg], axis=1).astype(BF16)
        bf_pad = jnp.pad(b_forget[l], (0, LANES - ATT_HEADS)).reshape(1, LANES)

        qkv, u, gates, qaug, kaug = _inproj(x, ada[l], g_mix[l].reshape(1, d), w_all,
                                      b_gate[l].reshape(1, 2 * d), bf_pad,
                                      tm=ti, att_w=att_w, ssm_w=ssm_w)

        att = _attention(qkv, qaug, kaug, tq=tq)

        s5w = _s5_operators(lam_re[l], lam_im[l], log_dt[l], b_re[l], b_im[l], c_re[l], c_im[l])
        ys = _s5(u, (*s5w, d_skip[l].astype(F32).reshape(-1, 1, LANES)), bb=bb)

        moe = l % 2 == 1
        m = l // 2
        if moe:
            wr = jnp.pad(w_router[m], ((0, 0), (0, LANES - N_EXPERTS)))
            br = jnp.pad(b_router[m], (0, LANES - N_EXPERTS)).reshape(1, LANES)
        else:
            wr = jnp.zeros((d, LANES), F32)
            br = jnp.zeros((1, LANES), F32)
        wr_hi = wr.astype(BF16)
        wr_split = jnp.concatenate([wr_hi, (wr - wr_hi.astype(F32)).astype(BF16)], axis=1)
        mixed = _mix(att, ys, gates, x, ada[l],
                     w_proj_att[l].astype(BF16), w_glu[l].astype(BF16),
                     b_glu[l].reshape(1, ssm_w), w_proj_ssm[l].astype(BF16),
                     w_out[l].astype(BF16), g_ffn[l].reshape(1, d), wr_split, br,
                     tm=tm, group=(tf // tm if moe else 0))
        x, h2 = mixed[0].reshape(bsz * seq, d), mixed[1].reshape(bsz * seq, d)
        last = l == depth - 1
        if moe:
            x = _moe(h2, mixed[2], mixed[3], x, ada[l],
                     w1_moe[m].astype(BF16), w3_moe[m].astype(BF16), w2_moe[m].astype(BF16), gfin,
                     tf=tf, rows=rows, seq=seq, final_norm=last)
        else:
            x = _ffn(h2, x, ada[l], w1_dense[m].astype(BF16), w3_dense[m].astype(BF16),
                     w2_dense[m].astype(BF16), gfin, tm=tf, cf=cf, seq=seq, final_norm=last)
        x = x.reshape(bsz, seq, d)
    return x
```

```python
import functools
import math

import jax
import jax.numpy as jnp
from jax import lax
from jax.experimental import pallas as pl
from jax.experimental.pallas import tpu as pltpu

F32 = jnp.float32
BF16 = jnp.bfloat16
EPS = 1e-6

ATT_HEADS = 8
ATT_HEAD_DIM = 64
SSM_GROUP = 16
SSM_STATE = 64
N_EXPERTS = 8
LANES = 128
SUB = 8
LOG2E = math.log2(math.e)
F_OFF = 32
VMEM_LIMIT = 56 * 1024 * 1024


def _params(sem):
    return pltpu.CompilerParams(dimension_semantics=sem, vmem_limit_bytes=VMEM_LIMIT)


def _dot(a, b):
    return jnp.dot(a, b, preferred_element_type=F32)


def _ada_kernel(c_ref, w_ref, b_ref, o_ref):
    ch, cl = _split(jax.nn.silu(c_ref[...]))
    wh, wl = _split(w_ref[0])
    o_ref[0] = (_dot(ch, wh) + _dot(ch, wl) + _dot(cl, wh)) + b_ref[0]


def _ada(c, w_ada, b_ada, tn=1536):
    depth, d, n = w_ada.shape
    bsz = c.shape[0]
    return pl.pallas_call(
        _ada_kernel,
        grid=(depth, n // tn),
        in_specs=[pl.BlockSpec((bsz, d), lambda l, j: (0, 0)),
                  pl.BlockSpec((1, d, tn), lambda l, j: (l, 0, j)),
                  pl.BlockSpec((1, 1, tn), lambda l, j: (l, 0, j))],
        out_specs=pl.BlockSpec((1, bsz, tn), lambda l, j: (l, 0, j)),
        out_shape=jax.ShapeDtypeStruct((depth, bsz, n), F32),
        compiler_params=_params(("arbitrary", "arbitrary")),
        name="ada",
    )(c, w_ada, b_ada.reshape(depth, 1, n))


def _modulated_norm(x, g, shift, scale):
    y = x * lax.rsqrt(jnp.mean(x * x, axis=-1, keepdims=True) + EPS) * g
    return y * (1.0 + scale) + shift


def _inproj_kernel(x_ref, ada_ref, g_ref, w_ref, bg_ref, bf_ref,
                   qkv_ref, u_ref, gate_ref, qa_ref, ka_ref, carry_ref, *, tm, att_w, ssm_w, d):
    j = pl.program_id(1)
    ada = ada_ref[0]
    h = _modulated_norm(x_ref[0], g_ref[...], ada[0:1], ada[1:2]).astype(BF16)

    cw = att_w
    qkv_ref[0, :, 0:cw] = (_dot(h, w_ref[:, 0:cw]) * (ATT_HEAD_DIM ** -0.5 * LOG2E)).astype(BF16)
    for c0 in range(cw, 3 * att_w, cw):
        qkv_ref[0, :, c0:c0 + cw] = _dot(h, w_ref[:, c0:c0 + cw]).astype(BF16)
    o_u = 3 * att_w
    u_ref[0] = _dot(h, w_ref[:, o_u:o_u + ssm_w])
    o_g = o_u + ssm_w
    for c0 in range(0, 2 * d, cw):
        z = _dot(h, w_ref[:, o_g + c0:o_g + c0 + cw]) + bg_ref[:, c0:c0 + cw]
        gate_ref[0, :, c0:c0 + cw] = jax.nn.sigmoid(z).astype(BF16)

    o_f = o_g + 2 * d
    logf = jax.nn.log_sigmoid(_dot(h, w_ref[:, o_f:o_f + LANES]) + bf_ref[...])
    row = lax.broadcasted_iota(jnp.int32, (tm, tm), 0)
    col = lax.broadcasted_iota(jnp.int32, (tm, tm), 1)
    tri = (row >= col).astype(BF16)
    hi = logf.astype(BF16)
    r1 = logf - hi.astype(F32)
    mid = r1.astype(BF16)
    lo = (r1 - mid.astype(F32)).astype(BF16)
    cs = _dot(tri, hi) + _dot(tri, mid) + _dot(tri, lo)

    @pl.when(j == 0)
    def _():
        carry_ref[...] = jnp.zeros_like(carry_ref)

    fcum = cs + carry_ref[0:1, :]
    carry_ref[...] = jnp.broadcast_to(fcum[tm - 1:tm, :], carry_ref.shape)

    lane = lax.broadcasted_iota(jnp.int32, (tm, LANES), 1)
    fl = jnp.where(lane < ATT_HEADS, fcum * LOG2E, 0.0)
    p0 = fl.astype(BF16).astype(F32)
    r0 = fl - p0
    p1 = r0.astype(BF16).astype(F32)
    p2 = (r0 - p1).astype(BF16).astype(F32)
    comb = p0 + pltpu.roll(p1, ATT_HEADS, 1) + pltpu.roll(p2, 2 * ATT_HEADS, 1)
    sel_lo = (lane < 3 * ATT_HEADS).astype(F32)
    sel_hi = ((lane >= F_OFF) & (lane < F_OFF + 3 * ATT_HEADS)).astype(F32)
    qa_ref[0] = (pltpu.roll(comb, F_OFF, 1) + sel_lo).astype(BF16)
    ka_ref[0] = (sel_hi - comb).astype(BF16)


def _inproj(x, ada_l, g, w_all, b_gate, b_forget_pad, *, tm, att_w, ssm_w):
    bsz, seq, d = x.shape
    n_all = w_all.shape[1]
    kern = functools.partial(_inproj_kernel, tm=tm, att_w=att_w, ssm_w=ssm_w, d=d)
    return pl.pallas_call(
        kern,
        grid=(bsz, seq // tm),
        in_specs=[pl.BlockSpec((1, tm, d), lambda b, j: (b, j, 0)),
                  pl.BlockSpec((1, 6, d), lambda b, j: (b, 0, 0)),
                  pl.BlockSpec((1, d), lambda b, j: (0, 0)),
                  pl.BlockSpec((d, n_all), lambda b, j: (0, 0)),
                  pl.BlockSpec((1, 2 * d), lambda b, j: (0, 0)),
                  pl.BlockSpec((1, LANES), lambda b, j: (0, 0))],
        out_specs=[pl.BlockSpec((1, tm, 3 * att_w), lambda b, j: (b, j, 0)),
                   pl.BlockSpec((1, tm, ssm_w), lambda b, j: (b, j, 0)),
                   pl.BlockSpec((1, tm, 2 * d), lambda b, j: (b, j, 0)),
                   pl.BlockSpec((1, tm, LANES), lambda b, j: (b, j, 0)),
                   pl.BlockSpec((1, tm, LANES), lambda b, j: (b, j, 0))],
        out_shape=[jax.ShapeDtypeStruct((bsz, seq, 3 * att_w), BF16),
                   jax.ShapeDtypeStruct((bsz, seq, ssm_w), F32),
                   jax.ShapeDtypeStruct((bsz, seq, 2 * d), BF16),
                   jax.ShapeDtypeStruct((bsz, seq, LANES), BF16),
                   jax.ShapeDtypeStruct((bsz, seq, LANES), BF16)],
        scratch_shapes=[pltpu.VMEM((8, LANES), F32)],
        compiler_params=_params(("arbitrary", "arbitrary")),
        name="inproj",
    )(x, ada_l, g, w_all, b_gate, b_forget_pad)


def _attn_kernel(q_ref, qa_ref, k_ref, ka_ref, v_ref, o_ref, *, tq):
    hp = pl.program_id(1)
    i = pl.program_id(2)
    half = tq // 2
    lane = lax.broadcasted_iota(jnp.int32, (tq, LANES), 1)
    first = lane < ATT_HEAD_DIM
    q2 = q_ref[0]
    qa = qa_ref[0]
    zero = jnp.zeros_like(q2)
    heads = []
    for hh in range(2):
        qsel = jnp.where(first, q2, zero) if hh == 0 else jnp.where(first, zero, q2)
        asel = jnp.where((lane & (ATT_HEADS - 1)) == 2 * hp + hh, qa, zero)
        heads.append(jnp.concatenate([qsel, asel], axis=1))

    def kv(start, size):
        kj = jnp.concatenate([k_ref[0, pl.ds(start, size), :], ka_ref[0, pl.ds(start, size), :]], axis=1)
        v2 = v_ref[0, pl.ds(start, size), :]
        one = jnp.ones_like(v2)
        lk = lax.broadcasted_iota(jnp.int32, v2.shape, 1) < ATT_HEAD_DIM
        return kj, (jnp.where(lk, v2, one), jnp.where(lk, one, v2))

    def update(qh, kj, vh, m, acc, mask):
        s = lax.dot_general(qh, kj, (((1,), (1,)), ((), ())), preferred_element_type=F32)
        if mask is not None:
            s = jnp.where(mask, s, -jnp.inf)
        m_new = jnp.maximum(m, jnp.max(s, axis=-1, keepdims=True))
        p = jnp.exp2(s - m_new).astype(BF16)
        acc = jnp.exp2(m - m_new) * acc + _dot(p, vh)
        return m_new, acc

    def body(j, carry):
        kj, vs = kv(pl.multiple_of(j * tq, tq), tq)
        out = []
        for hh in range(2):
            out.extend(update(heads[hh], kj, vs[hh], carry[2 * hh], carry[2 * hh + 1], None))
        return tuple(out)

    init = []
    for hh in range(2):
        init += [jnp.full((tq, 1), -jnp.inf, F32), jnp.zeros((tq, LANES), F32)]
    carry = lax.fori_loop(0, i, body, tuple(init))

    base = pl.multiple_of(i * tq, tq)
    r_top = lax.broadcasted_iota(jnp.int32, (half, half), 0)
    c_top = lax.broadcasted_iota(jnp.int32, (half, half), 1)
    r_bot = lax.broadcasted_iota(jnp.int32, (half, tq), 0)
    c_bot = lax.broadcasted_iota(jnp.int32, (half, tq), 1)
    k_top, v_top = kv(base, half)
    k_all, v_all = kv(base, tq)
    outs = []
    for hh in range(2):
        m, acc = carry[2 * hh], carry[2 * hh + 1]
        qh = heads[hh]
        _, acc_t = update(qh[:half], k_top, v_top[hh], m[:half], acc[:half], r_top >= c_top)
        _, acc_b = update(qh[half:], k_all, v_all[hh], m[half:], acc[half:], r_bot + half >= c_bot)
        acc = jnp.concatenate([acc_t, acc_b], axis=0)
        outs.append(acc / pltpu.roll(acc, ATT_HEAD_DIM, 1))
    o_ref[0] = jnp.where(first, outs[0], outs[1]).astype(BF16)


def _attention(qkv, qaug, kaug, *, tq):
    bsz, seq, w3 = qkv.shape
    att_w = w3 // 3
    npair = att_w // LANES
    kern = functools.partial(_attn_kernel, tq=tq)
    return pl.pallas_call(
        kern,
        grid=(bsz, npair, seq // tq),
        in_specs=[pl.BlockSpec((1, tq, LANES), lambda b, p, i: (b, i, p)),
                  pl.BlockSpec((1, tq, LANES), lambda b, p, i: (b, i, 0)),
                  pl.BlockSpec((1, seq, LANES), lambda b, p, i: (b, 0, npair + p)),
                  pl.BlockSpec((1, seq, LANES), lambda b, p, i: (b, 0, 0)),
                  pl.BlockSpec((1, seq, LANES), lambda b, p, i: (b, 0, 2 * npair + p))],
        out_specs=pl.BlockSpec((1, tq, LANES), lambda b, p, i: (b, i, p)),
        out_shape=jax.ShapeDtypeStruct((bsz, seq, att_w), BF16),
        compiler_params=_params(("arbitrary", "arbitrary", "arbitrary")),
        name="fox_attention",
    )(qkv, qaug, qkv, kaug, qkv)


def _split(x):
    hi = x.astype(BF16)
    return hi, (x - hi.astype(F32)).astype(BF16)


def _dot_nt3(a, b):
    dims = (((1,), (1,)), ((), ()))
    ah, al = _split(a)
    bh, bl = _split(b)
    nt = lambda u, v: lax.dot_general(u, v, dims, preferred_element_type=F32)
    return nt(ah, bh) + nt(ah, bl) + nt(al, bh)


def _dot_rep3(x, rep):
    p0 = x.astype(BF16)
    r0 = x - p0.astype(F32)
    p1 = r0.astype(BF16)
    p2 = (r0 - p1.astype(F32)).astype(BF16)
    return _dot(p0, rep) + _dot(p1, rep) + _dot(p2, rep)


def _s5_operator_kernel(lr_ref, li_ref, dt_ref, br_ref, bi_ref, cr_ref, ci_ref,
                        t_ref, b_ref, ct_ref, a1_ref, a2_ref, *, hc, p):
    gl = LANES // hc
    half = gl * p
    lr, li, dt = lr_ref[0], li_ref[0], jnp.exp(dt_ref[0])
    br, bi, cr, ci = br_ref[0], bi_ref[0], cr_ref[0], ci_ref[0]
    mag = jnp.exp(lr * dt)
    ang = li * dt
    a_re, a_im = mag * jnp.cos(ang), mag * jnp.sin(ang)
    n_re, n_im = a_re - 1.0, a_im
    den = lr * lr + li * li
    k_re = (n_re * lr + n_im * li) / den
    k_im = (n_im * lr - n_re * li) / den
    bb_re = k_re * br - k_im * bi
    bb_im = k_re * bi + k_im * br

    def power(t):
        m = jnp.exp(lr * dt * t)
        return m * jnp.cos(ang * t), m * jnp.sin(ang * t)

    pw = [power(float(t)) for t in range(SUB + 1)]

    rep = (lax.broadcasted_iota(jnp.int32, (p, half), 1) % p
           == lax.broadcasted_iota(jnp.int32, (p, half), 0)).astype(BF16)
    own = (lax.broadcasted_iota(jnp.int32, (LANES, half), 0) // hc
           == lax.broadcasted_iota(jnp.int32, (LANES, half), 1) // p)

    def expand(x):
        return jnp.where(own, _dot(x.astype(BF16), rep), 0.0).astype(BF16)

    same = (lax.broadcasted_iota(jnp.int32, (LANES, LANES), 0) // hc
            == lax.broadcasted_iota(jnp.int32, (LANES, LANES), 1) // hc)
    zero = jnp.zeros((LANES, LANES), BF16)
    kern = []
    for t in range(SUB):
        pr, pi = pw[t]
        wr, wi = cr * pr - ci * pi, cr * pi + ci * pr
        k = _dot_nt3(bb_re, wr) - _dot_nt3(bb_im, wi)
        kern.append(jnp.where(same, k, 0.0).astype(BF16))
    for j in range(SUB):
        for i in range(SUB):
            t_ref[0, j * LANES:(j + 1) * LANES, i * LANES:(i + 1) * LANES] = kern[i - j] if i >= j else zero
        pr, pi = pw[SUB - 1 - j]
        b_ref[0, j * LANES:(j + 1) * LANES, 0:half] = expand(pr * bb_re - pi * bb_im)
        b_ref[0, j * LANES:(j + 1) * LANES, half:2 * half] = expand(pr * bb_im + pi * bb_re)
        pr, pi = pw[j + 1]
        ct_ref[0, j * LANES:(j + 1) * LANES, 0:half] = expand(cr * pr - ci * pi)
        ct_ref[0, j * LANES:(j + 1) * LANES, half:2 * half] = expand(-(cr * pi + ci * pr))
    pr, pi = pw[SUB]
    along = lambda x: jnp.sum(jnp.where(own, _dot_rep3(x, rep), 0.0), axis=0, keepdims=True) * (1.0 / hc)
    ac_re, ac_im = along(pr), along(pi)
    a1_ref[0] = jnp.concatenate([ac_re, ac_re], axis=1)
    a2_ref[0] = jnp.concatenate([-ac_im, ac_im], axis=1)


def _s5_operators(lam_re, lam_im, log_dt, b_re, b_im, c_re, c_im):
    g, p = lam_re.shape
    hc = b_re.shape[-1]
    nt = g * hc // LANES
    ch = SUB * LANES
    ns = 2 * (LANES // hc) * p
    rows = lambda v: jnp.broadcast_to(v[:, None, :], (g, hc, v.shape[-1])).reshape(nt, LANES, v.shape[-1])
    compact = [rows(lam_re.astype(F32)), rows(lam_im.astype(F32)),
               rows(jnp.broadcast_to(log_dt.astype(F32)[:, None], (g, p))),
               b_re.astype(F32).transpose(0, 2, 1).reshape(nt, LANES, p),
               b_im.astype(F32).transpose(0, 2, 1).reshape(nt, LANES, p),
               c_re.astype(F32).reshape(nt, LANES, p), c_im.astype(F32).reshape(nt, LANES, p)]
    kern = functools.partial(_s5_operator_kernel, hc=hc, p=p)
    return pl.pallas_call(
        kern,
        grid=(nt,),
        in_specs=[pl.BlockSpec((1, LANES, p), lambda t: (t, 0, 0))] * 7,
        out_specs=[pl.BlockSpec((1, ch, ch), lambda t: (t, 0, 0)),
                   pl.BlockSpec((1, ch, ns), lambda t: (t, 0, 0)),
                   pl.BlockSpec((1, ch, ns), lambda t: (t, 0, 0)),
                   pl.BlockSpec((1, 1, ns), lambda t: (t, 0, 0)),
                   pl.BlockSpec((1, 1, ns), lambda t: (t, 0, 0))],
        out_shape=[jax.ShapeDtypeStruct((nt, ch, ch), BF16), jax.ShapeDtypeStruct((nt, ch, ns), BF16),
                   jax.ShapeDtypeStruct((nt, ch, ns), BF16), jax.ShapeDtypeStruct((nt, 1, ns), F32),
                   jax.ShapeDtypeStruct((nt, 1, ns), F32)],
        compiler_params=_params(("arbitrary",)),
        name="s5_operators",
    )(*compact)


def _s5_kernel(u_ref, t_ref, b_ref, ct_ref, a1_ref, a2_ref, d_ref, y_ref, uf_ref, s_ref, *, bb, nk):
    nc = s_ref.shape[0]
    for b in range(bb):
        for j in range(SUB):
            uf_ref[b * nk:(b + 1) * nk, j * LANES:(j + 1) * LANES] = (
                u_ref[b, pl.ds(j, nk, stride=SUB), :].astype(BF16))
    pitch = s_ref.shape[1] // bb
    for b in range(bb):
        s_loc = _dot(uf_ref[b * nk:(b + 1) * nk, :], b_ref[0])
        for c in range(nc):
            s_ref[c, b * pitch:b * pitch + nk, :] = s_loc[:, c * LANES:(c + 1) * LANES]
    a1 = a1_ref[0]
    a2 = a2_ref[0]

    def carry_step(k, state):
        rows = pl.ds(k, bb, stride=pitch)
        loc = jnp.concatenate([s_ref[c, rows, :] for c in range(nc)], axis=1)
        for c in range(nc):
            s_ref[c, rows, :] = state[:, c * LANES:(c + 1) * LANES]
        return a1 * state + a2 * pltpu.roll(state, nc * LANES // 2, 1) + loc

    lax.fori_loop(0, nk, carry_step, jnp.zeros((bb, nc * LANES), F32), unroll=4)
    d = d_ref[0]
    for b in range(bb):
        rows = slice(b * nk, (b + 1) * nk)
        s_prev = jnp.concatenate([s_ref[c, b * pitch:b * pitch + nk, :] for c in range(nc)],
                                 axis=1).astype(BF16)
        y = _dot(uf_ref[rows, :], t_ref[0]) + lax.dot_general(
            s_prev, ct_ref[0], (((1,), (1,)), ((), ())), preferred_element_type=F32)
        for i in range(SUB):
            yi = y[:, i * LANES:(i + 1) * LANES] + d * u_ref[b, pl.ds(i, nk, stride=SUB), :]
            y_ref[b, pl.ds(i, nk, stride=SUB), :] = jax.nn.gelu(yi)


def _s5(u, weights, *, bb):
    bsz, seq, w = u.shape
    t_intra, b_state, c_state, a1, a2, dvec = weights
    nt = t_intra.shape[0]
    nk = seq // SUB
    ch = SUB * LANES
    ns = b_state.shape[-1]
    kern = functools.partial(_s5_kernel, bb=bb, nk=nk)
    return pl.pallas_call(
        kern,
        grid=(nt, bsz // bb),
        in_specs=[pl.BlockSpec((bb, seq, LANES), lambda t, i: (i, 0, t)),
                  pl.BlockSpec((1, ch, ch), lambda t, i: (t, 0, 0)),
                  pl.BlockSpec((1, ch, ns), lambda t, i: (t, 0, 0)),
                  pl.BlockSpec((1, ns, ch), lambda t, i: (t, 0, 0)),
                  pl.BlockSpec((1, 1, ns), lambda t, i: (t, 0, 0)),
                  pl.BlockSpec((1, 1, ns), lambda t, i: (t, 0, 0)),
                  pl.BlockSpec((1, 1, LANES), lambda t, i: (t, 0, 0))],
        out_specs=pl.BlockSpec((bb, seq, LANES), lambda t, i: (i, 0, t)),
        out_shape=jax.ShapeDtypeStruct((bsz, seq, w), F32),
        scratch_shapes=[pltpu.VMEM((bb * nk, ch), BF16),
                        pltpu.VMEM((ns // LANES, bb * (nk + SUB * (1 + nk // SUB % 2)), LANES), F32)],
        compiler_params=_params(("arbitrary", "arbitrary")),
        name="s5_chunked",
    )(u, t_intra, b_state, c_state, a1, a2, dvec)


def _route(lg):
    n = lg.shape[0]
    sub = lax.broadcasted_iota(jnp.int32, lg.shape, 0)
    neg = jnp.float32(-jnp.inf)
    m1 = jnp.max(lg, axis=0, keepdims=True)
    i1 = jnp.min(jnp.where(lg == m1, sub, n), axis=0, keepdims=True)
    lg2 = jnp.where(sub == i1, neg, lg)
    m2 = jnp.max(lg2, axis=0, keepdims=True)
    i2 = jnp.min(jnp.where(lg2 == m2, sub, n), axis=0, keepdims=True)
    e2 = jnp.exp(m2 - m1)
    den = 1.0 + e2
    gates = jnp.where(sub == i1, 1.0 / den, 0.0) + jnp.where(sub == i2, e2 / den, 0.0)
    return gates, (sub == i1) | (sub == i2)


def _mix_kernel(att_ref, ys_ref, gate_ref, x_ref, ada_ref, wpa_ref, wglu_ref, bglu_ref, wps_ref,
                wout_ref, gffn_ref, wr_ref, br_ref, xo_ref, h2_ref, *route_refs, d, tm, group):
    ada = ada_ref[0]
    ya = _dot(att_ref[0], wpa_ref[...])
    ys = ys_ref[0]
    glu = ys * jax.nn.sigmoid(_dot(ys.astype(BF16), wglu_ref[...]) + bglu_ref[...])
    yp = _dot(glu.astype(BF16), wps_ref[...])
    mixed = gate_ref[0, :, 0:d].astype(F32) * ya + gate_ref[0, :, d:2 * d].astype(F32) * yp
    x_new = x_ref[0] + ada[2:3] * _dot(mixed.astype(BF16), wout_ref[...])
    xo_ref[0] = x_new
    h2 = _modulated_norm(x_new, gffn_ref[...], ada[3:4], ada[4:5])
    h2b = h2.astype(BF16)
    h2_ref[0] = h2b
    if not route_refs:
        return
    rkr_ref, gwr_ref, cnt_ref = route_refs
    h2l = (h2 - h2b.astype(F32)).astype(BF16)
    both = _dot(h2b, wr_ref[...])
    logits = (both[:, 0:LANES] + both[:, LANES:2 * LANES] + _dot(h2l, wr_ref[:, 0:LANES])) + br_ref[...]
    gates, mask = _route(jnp.transpose(logits)[0:N_EXPERTS, :])

    @pl.when(pl.program_id(1) % group == 0)
    def _():
        cnt_ref[...] = jnp.zeros_like(cnt_ref)

    lane = lax.broadcasted_iota(jnp.int32, (N_EXPERTS, tm), 1)
    maskf = mask.astype(F32)
    incl = maskf
    shift = 1
    while shift < tm:
        incl = incl + jnp.where(lane >= shift, pltpu.roll(incl, shift, 1), 0.0)
        shift *= 2
    before = incl - maskf + cnt_ref[:, 0:1]
    rkr_ref[...] = jnp.where(mask, before, -1.0)
    gwr_ref[...] = gates
    cnt_ref[...] = jnp.broadcast_to(before[:, tm - 1:tm] + maskf[:, tm - 1:tm], cnt_ref.shape)


def _mix(att, ys, gates, x, ada_l, wpa, wglu, bglu, wps, wout, gffn, wr_split, b_router, *, tm, group):
    bsz, seq, d = x.shape
    aw = att.shape[-1]
    sw = ys.shape[-1]
    nj = seq // tm
    kern = functools.partial(_mix_kernel, d=d, tm=tm, group=group)
    const = lambda b, j: (0, 0)
    tile = lambda b, j: (b, j, 0)
    out_specs = [pl.BlockSpec((1, tm, d), tile), pl.BlockSpec((1, tm, d), tile)]
    out_shape = [jax.ShapeDtypeStruct((bsz, seq, d), F32), jax.ShapeDtypeStruct((bsz, seq, d), BF16)]
    scratch = []
    if group:
        flat = lambda b, j: (0, b * nj + j)
        out_specs += [pl.BlockSpec((N_EXPERTS, tm), flat), pl.BlockSpec((N_EXPERTS, tm), flat)]
        out_shape += [jax.ShapeDtypeStruct((N_EXPERTS, bsz * seq), F32),
                      jax.ShapeDtypeStruct((N_EXPERTS, bsz * seq), F32)]
        scratch = [pltpu.VMEM((N_EXPERTS, LANES), F32)]
    return pl.pallas_call(
        kern,
        grid=(bsz, nj),
        in_specs=[pl.BlockSpec((1, tm, aw), tile),
                  pl.BlockSpec((1, tm, sw), tile),
                  pl.BlockSpec((1, tm, 2 * d), tile),
                  pl.BlockSpec((1, tm, d), tile),
                  pl.BlockSpec((1, 6, d), lambda b, j: (b, 0, 0)),
                  pl.BlockSpec((aw, d), const),
                  pl.BlockSpec((sw, sw), const),
                  pl.BlockSpec((1, sw), const),
                  pl.BlockSpec((sw, d), const),
                  pl.BlockSpec((d, d), const),
                  pl.BlockSpec((1, d), const),
                  pl.BlockSpec((d, 2 * LANES), const),
                  pl.BlockSpec((1, LANES), const)],
        out_specs=out_specs,
        out_shape=out_shape,
        scratch_shapes=scratch,
        compiler_params=_params(("arbitrary", "arbitrary")),
        name="mix",
    )(att, ys, gates, x, ada_l, wpa, wglu, bglu, wps, wout, gffn, wr_split, b_router)


def _residual_out(x, ada, acc, gfin_ref, final_norm):
    y = x + ada[5:6] * acc
    if final_norm:
        y = y * lax.rsqrt(jnp.mean(y * y, axis=-1, keepdims=True) + EPS) * gfin_ref[...]
    return y


def _ffn_kernel(h_ref, x_ref, ada_ref, w1_ref, w3_ref, w2_ref, gfin_ref, o_ref, acc_ref, *, nf, final_norm):
    f = pl.program_id(1)

    @pl.when(f == 0)
    def _():
        acc_ref[...] = jnp.zeros_like(acc_ref)

    h = h_ref[...]
    act = jax.nn.silu(_dot(h, w1_ref[...])) * _dot(h, w3_ref[...])
    acc_ref[...] += _dot(act.astype(BF16), w2_ref[...])

    @pl.when(f == nf - 1)
    def _():
        o_ref[...] = _residual_out(x_ref[...], ada_ref[0], acc_ref[...], gfin_ref, final_norm)


def _ffn(h2, x, ada_l, w1, w3, w2, g_final, *, tm, cf, seq, final_norm):
    t, d = h2.shape
    dff = w1.shape[-1]
    nf = dff // cf
    tiles_per_seq = seq // tm
    kern = functools.partial(_ffn_kernel, nf=nf, final_norm=final_norm)
    return pl.pallas_call(
        kern,
        grid=(t // tm, nf),
        in_specs=[pl.BlockSpec((tm, d), lambda i, f: (i, 0)),
                  pl.BlockSpec((tm, d), lambda i, f: (i, 0)),
                  pl.BlockSpec((1, 6, d), lambda i, f: (i // tiles_per_seq, 0, 0)),
                  pl.BlockSpec((d, cf), lambda i, f: (0, f)),
                  pl.BlockSpec((d, cf), lambda i, f: (0, f)),
                  pl.BlockSpec((cf, d), lambda i, f: (f, 0)),
                  pl.BlockSpec((1, d), lambda i, f: (0, 0))],
        out_specs=pl.BlockSpec((tm, d), lambda i, f: (i, 0)),
        out_shape=jax.ShapeDtypeStruct((t, d), F32),
        scratch_shapes=[pltpu.VMEM((tm, d), F32)],
        compiler_params=_params(("arbitrary", "arbitrary")),
        name="ffn",
    )(h2, x, ada_l, w1, w3, w2, g_final)


def _moe_kernel(h_ref, rkr_ref, gwr_ref, x_ref, ada_ref, w1_ref, w3_ref, w2_ref, gfin_ref, o_ref,
                *, n_exp, rows, final_norm):
    e = pl.program_id(1)
    tf = h_ref.shape[0]

    @pl.when(e == 0)
    def _():
        o_ref[...] = jnp.zeros_like(o_ref)

    rk_row = rkr_ref[pl.ds(e, 1), :]
    gw_row = gwr_ref[pl.ds(e, 1), :]
    count = jnp.max(rk_row).astype(jnp.int32) + 1

    r_sub = lax.broadcasted_iota(jnp.int32, (rows, tf), 0).astype(F32)

    def chunk(c, carry):
        pick = rk_row == r_sub + (c * rows).astype(F32)
        pickb = pick.astype(BF16)
        xc = _dot(pickb, h_ref[...]).astype(BF16)
        gate = jnp.sum(jnp.where(pick, gw_row, 0.0), axis=-1, keepdims=True)
        act = jax.nn.silu(_dot(xc, w1_ref[0])) * _dot(xc, w3_ref[0]) * gate
        yc = _dot(act.astype(BF16), w2_ref[0]).astype(BF16)
        o_ref[...] += lax.dot_general(pickb, yc, (((0,), (0,)), ((), ())), preferred_element_type=F32)
        return carry

    lax.fori_loop(0, (count + rows - 1) // rows, chunk, 0)

    @pl.when(e == n_exp - 1)
    def _():
        o_ref[...] = _residual_out(x_ref[...], ada_ref[0], o_ref[...], gfin_ref, final_norm)


def _moe(h2, rk_row, gw_row, x, ada_l, w1, w3, w2, g_final, *, tf, rows, seq, final_norm):
    t, d = h2.shape
    n_exp, _, dff = w1.shape
    tiles_per_seq = seq // tf
    kern = functools.partial(_moe_kernel, n_exp=n_exp, rows=rows, final_norm=final_norm)
    return pl.pallas_call(
        kern,
        grid=(t // tf, n_exp),
        in_specs=[pl.BlockSpec((tf, d), lambda i, e: (i, 0)),
                  pl.BlockSpec((N_EXPERTS, tf), lambda i, e: (0, i)),
                  pl.BlockSpec((N_EXPERTS, tf), lambda i, e: (0, i)),
                  pl.BlockSpec((tf, d), lambda i, e: (i, 0)),
                  pl.BlockSpec((1, 6, d), lambda i, e: (i // tiles_per_seq, 0, 0)),
                  pl.BlockSpec((1, d, dff), lambda i, e: (e, 0, 0)),
                  pl.BlockSpec((1, d, dff), lambda i, e: (e, 0, 0)),
                  pl.BlockSpec((1, dff, d), lambda i, e: (e, 0, 0)),
                  pl.BlockSpec((1, d), lambda i, e: (0, 0))],
        out_specs=pl.BlockSpec((tf, d), lambda i, e: (i, 0)),
        out_shape=jax.ShapeDtypeStruct((t, d), F32),
        compiler_params=_params(("arbitrary", "arbitrary")),
        name="moe",
    )(h2, rk_row, gw_row, x, ada_l, w1, w3, w2, g_final)


def _tiles(seq, bsz):
    ti = min(256, seq)
    tm = min(1024, seq)
    tq = min(1024, seq // 2)
    tf = min(1024, seq)
    return ti, tm, tq, min(4, bsz), tf, 144


def kernel(x, c, w_ada, b_ada, g_mix, w_in, b_forget, b_gate, lam_re, lam_im, log_dt, b_re, b_im,
           c_re, c_im, d_skip, w_glu, b_glu, w_proj_att, w_proj_ssm, w_out, g_ffn, w1_dense, w3_dense,
           w2_dense, w_router, b_router, w1_moe, w3_moe, w2_moe, g_final):
    bsz, seq, d = x.shape
    depth = w_ada.shape[0]
    att_w = ATT_HEADS * ATT_HEAD_DIM
    ssm_w = w_glu.shape[-1]
    dff = w1_dense.shape[-1]
    ti, tm, tq, bb, tf, rows = _tiles(seq, bsz)
    cf = dff // 2

    ada = _ada(c, w_ada, b_ada).reshape(depth, bsz, 6, d)
    gfin = g_final.reshape(1, d)

    for l in range(depth):
        o1, o2, o3 = 3 * att_w, 3 * att_w + ATT_HEADS, 3 * att_w + ATT_HEADS + ssm_w
        w = w_in[l]
        w_fg = jnp.pad(w[:, o1:o2], ((0, 0), (0, LANES - ATT_HEADS)))
        w_all = jnp.concatenate([w[:, :o1], w[:, o2:o3], w[:, o3:], w_fg], axis=1).astype(BF16)
        bf_pad = jnp.pad(b_forget[l], (0, LANES - ATT_HEADS)).reshape(1, LANES)

        qkv, u, gates, qaug, kaug = _inproj(x, ada[l], g_mix[l].reshape(1, d), w_all,
                                      b_gate[l].reshape(1, 2 * d), bf_pad,
                                      tm=ti, att_w=att_w, ssm_w=ssm_w)

        att = _attention(qkv, qaug, kaug, tq=tq)

        s5w = _s5_operators(lam_re[l], lam_im[l], log_dt[l], b_re[l], b_im[l], c_re[l], c_im[l])
        ys = _s5(u, (*s5w, d_skip[l].astype(F32).reshape(-1, 1, LANES)), bb=bb)

        moe = l % 2 == 1
        m = l // 2
        if moe:
            wr = jnp.pad(w_router[m], ((0, 0), (0, LANES - N_EXPERTS)))
            br = jnp.pad(b_router[m], (0, LANES - N_EXPERTS)).reshape(1, LANES)
        else:
            wr = jnp.zeros((d, LANES), F32)
            br = jnp.zeros((1, LANES), F32)
        wr_hi = wr.astype(BF16)
        wr_split = jnp.concatenate([wr_hi, (wr - wr_hi.astype(F32)).astype(BF16)], axis=1)
        mixed = _mix(att, ys, gates, x, ada[l],
                     w_proj_att[l].astype(BF16), w_glu[l].astype(BF16),
                     b_glu[l].reshape(1, ssm_w), w_proj_ssm[l].astype(BF16),
                     w_out[l].astype(BF16), g_ffn[l].reshape(1, d), wr_split, br,
                     tm=tm, group=(tf // tm if moe else 0))
        x, h2 = mixed[0].reshape(bsz * seq, d), mixed[1].reshape(bsz * seq, d)
        last = l == depth - 1
        if moe:
            x = _moe(h2, mixed[2], mixed[3], x, ada[l],
                     w1_moe[m].astype(BF16), w3_moe[m].astype(BF16), w2_moe[m].astype(BF16), gfin,
                     tf=tf, rows=rows, seq=seq, final_norm=last)
        else:
            x = _ffn(h2, x, ada[l], w1_dense[m].astype(BF16), w3_dense[m].astype(BF16),
                     w2_dense[m].astype(BF16), gfin, tm=tf, cf=cf, seq=seq, final_norm=last)
        x = x.reshape(bsz, seq, d)
    return x
```
